```python
import math
import numpy as np
import jax
import jax.numpy as jnp
from jax import lax


D_MODEL = 1024
BATCH = 16
SEQ = 4096
DEPTH = 1
DEC_BATCH = 128
DEC_SEQ = 1
PAST_LEN = 8192
PAGE_SIZE = 128

GLA_HEADS = 4
GLA_DK = D_MODEL // 2 // GLA_HEADS
GLA_DV = D_MODEL // GLA_HEADS
GLA_QK_W = GLA_HEADS * GLA_DK
GLA_V_W = GLA_HEADS * GLA_DV
GLA_GATE_RANK = 16
GLA_GATE_NORM = 16.0
GLA_CHUNK = 64
ATT_GROUPS = ((128, 1), (512, 4), (2048, 16))
ATT_HEADS_PER_GROUP = 8
ATT_HEAD_DIM = 64
ATT_HEADS = ATT_HEADS_PER_GROUP * len(ATT_GROUPS)
ATT_W = ATT_HEADS * ATT_HEAD_DIM
ATT_OUT_W = ATT_HEADS_PER_GROUP * ATT_HEAD_DIM
ATT_BLOCK = 128
REL_BUCKETS = 32
REL_MAX_DIST = 2048
FFN_HIDDEN = 2816
CONV_W = 3
N_ADA = 6
EPS = 1e-6

SPLIT_SIZES = (GLA_QK_W, GLA_QK_W, GLA_V_W, GLA_V_W, GLA_GATE_RANK, ATT_W, ATT_W, ATT_W, D_MODEL, D_MODEL)
SPLIT_IDX = [int(s) for s in np.cumsum(SPLIT_SIZES)[:-1]]

kernel_name = 'hybrid_gla_dilated_adaln_step'


def _rmsnorm(x, g):
    xf = x.astype(jnp.float32)
    y = xf * lax.rsqrt(jnp.mean(xf * xf, axis=-1, keepdims=True) + EPS)
    return (y * g.astype(jnp.float32)).astype(x.dtype)


def _modulate(h, shift, scale):
    return h * (1 + scale[:, None]) + shift[:, None]


def _t5_bucket(dist):
    max_exact = REL_BUCKETS // 2
    d = np.maximum(dist, 1).astype(np.float32)
    large = max_exact + (np.log(d / max_exact) / np.log(REL_MAX_DIST / max_exact) * (REL_BUCKETS - max_exact)).astype(np.int32)
    large = np.minimum(large, REL_BUCKETS - 1)
    return np.where(dist < max_exact, dist, large).astype(np.int32)


def _gla(q, k, v, logg, s0):
    B, T, H, _ = q.shape
    C = min(GLA_CHUNK, T)
    n = -(-T // C)
    pad = n * C - T

    def blocks(t):
        t = jnp.pad(t, ((0, 0), (0, pad), (0, 0), (0, 0)))
        return jnp.moveaxis(t.reshape(B, n, C, H, t.shape[-1]), 1, 0)

    mask = jnp.tril(jnp.ones((C, C), dtype=bool))

    def step(S, inp):
        qc, kc, vc, gc = inp
        b = jnp.cumsum(gc, axis=1)
        ref = b[:, C // 2:C // 2 + 1]
        a = jnp.einsum('bihd,bjhd->bhij', qc * jnp.exp(b - ref), kc * jnp.exp(ref - b))
        a = jnp.where(mask, a, 0.0)
        o = jnp.einsum('bhij,bjhv->bihv', a, vc) + jnp.einsum('bihd,bhdv->bihv', qc * jnp.exp(b), S)
        blast = b[:, -1]
        kd = kc * jnp.exp(blast[:, None] - b)
        S = S * jnp.exp(blast)[..., None] + jnp.einsum('bjhd,bjhv->bhdv', kd, vc)
        return S, o

    S, o = lax.scan(step, s0, (blocks(q), blocks(k), blocks(v), blocks(logg)))
    o = jnp.moveaxis(o, 0, 1).reshape(B, n * C, H, v.shape[-1])[:, :T]
    return o, S


def _softmax_lse(s, axis=-1):
    m = jnp.max(s, axis=axis, keepdims=True)
    p = jnp.exp(s - m)
    den = jnp.sum(p, axis=axis, keepdims=True)
    return p / den, (m + jnp.log(den))


def _dilated_prompt(q, k, v, rel_bias_g, window, dil):
    B, T, H, E = q.shape
    span = window // dil
    L = T // dil
    nb = -(-L // ATT_BLOCK)
    Lp = nb * ATT_BLOCK

    def to_blocks(t):
        t = t.reshape(B, L, dil, H, E).transpose(0, 2, 1, 3, 4)
        t = jnp.pad(t, ((0, 0), (0, 0), (0, Lp - L), (0, 0), (0, 0)))
        return t.reshape(B, dil, nb, ATT_BLOCK, H, E)

    def with_prev(t):
        prev = jnp.pad(t, ((0, 0), (0, 0), (1, 0), (0, 0), (0, 0), (0, 0)))[:, :, :-1]
        return jnp.concatenate([prev, t], axis=3)

    qb = to_blocks(q)
    kk = with_prev(to_blocks(k))
    vv = with_prev(to_blocks(v))
    i = np.arange(ATT_BLOCK)[:, None]
    j = np.arange(2 * ATT_BLOCK)[None, :]
    off = ATT_BLOCK + i - j
    valid = (off >= 0) & (off <= span)
    bucket = _t5_bucket(np.clip(off, 0, span) * dil)
    bias = jnp.transpose(rel_bias_g[bucket], (2, 0, 1))
    first_blk = (np.arange(nb)[:, None, None] == 0) & (j[None] < ATT_BLOCK)
    mask = valid[None] & ~first_blk
    s = jnp.einsum('brnihe,brnjhe->brnhij', qb, kk) * (E ** -0.5) + bias[None, None, None]
    s = jnp.where(mask[None, None, :, None], s, -jnp.inf)
    p, lse = _softmax_lse(s)
    o = jnp.einsum('brnhij,brnjhe->brnihe', p, vv)
    o = o.reshape(B, dil, Lp, H, E)[:, :, :L].transpose(0, 2, 1, 3, 4).reshape(B, T, H, E)
    lse = lse[..., 0].transpose(0, 1, 2, 4, 3).reshape(B, dil, Lp, H)[:, :, :L]
    lse = lse.transpose(0, 2, 1, 3).reshape(B, T, H)
    return o, lse


def _dilated_sample(q, k, v, buf, rel_bias_g, window, dil):
    B, Tn, H, E = q.shape
    Lb = buf.shape[1]
    span = window // dil
    kall = jnp.concatenate([buf[:, :, 0].astype(jnp.float32), k], axis=1)
    vall = jnp.concatenate([buf[:, :, 1].astype(jnp.float32), v], axis=1)
    mstep = np.arange(span + 1)
    idx = Lb + np.arange(Tn)[:, None] - mstep[None, :] * dil
    valid = idx >= 0
    idx = np.maximum(idx, 0)
    kg = kall[:, idx]
    vg = vall[:, idx]
    bias = jnp.transpose(rel_bias_g[_t5_bucket(mstep * dil)], (1, 0))
    s = jnp.einsum('bthe,btshe->bths', q, kg) * (E ** -0.5) + bias[None, None]
    s = jnp.where(valid[None, :, None, :], s, -jnp.inf)
    p, lse = _softmax_lse(s)
    o = jnp.einsum('bths,btshe->bthe', p, vg)
    new_buf = jnp.stack([kall, vall], axis=2)[:, kall.shape[1] - Lb:].astype(buf.dtype)
    return o, lse[..., 0], new_buf


def _conv_ffn(h, conv_state, w_up, conv_w, conv_b, w_down):
    T = h.shape[1]
    a, b = jnp.split(h @ w_up, 2, axis=-1)
    a_ext = jnp.concatenate([conv_state.astype(a.dtype), a], axis=1)
    a_conv = conv_b + sum(conv_w[j] * a_ext[:, j:j + T] for j in range(CONV_W))
    y = (jax.nn.silu(a_conv) * b) @ w_down
    return y, a_ext[:, T:]


def _layer(x, c, gla_state, win_bufs, conv_state, rel_bias,
           w_ada, b_ada, norm1_g, w_in, gla_wg2, gla_bg, gla_norm_g,
           w_branch_a, w_branch_b, w_out, norm2_g, w_up, conv_w, conv_b, w_down):
    B, T, _ = x.shape
    f32 = jnp.float32
    shift1, scale1, gate1, shift2, scale2, gate2 = jnp.split(jax.nn.silu(c) @ w_ada + b_ada, N_ADA, axis=-1)
    h = _modulate(_rmsnorm(x, norm1_g), shift1, scale1)
    gq, gk, gv, gr, glr, aq, ak, av, ga, gb = jnp.split(h @ w_in, SPLIT_IDX, axis=-1)

    q = gq.reshape(B, T, GLA_HEADS, GLA_DK).astype(f32) * (GLA_DK ** -0.5)
    k = gk.reshape(B, T, GLA_HEADS, GLA_DK).astype(f32)
    v = gv.reshape(B, T, GLA_HEADS, GLA_DV).astype(f32)
    logg = jax.nn.log_sigmoid((glr @ gla_wg2 + gla_bg).astype(f32)).reshape(B, T, GLA_HEADS, GLA_DK) / GLA_GATE_NORM
    s0 = jnp.zeros((B, GLA_HEADS, GLA_DK, GLA_DV), f32) if gla_state is None else gla_state.astype(f32)
    o_gla, s_gla = _gla(q, k, v, logg, s0)
    o_gla = _rmsnorm(o_gla, gla_norm_g) * jax.nn.silu(gr.astype(f32)).reshape(B, T, GLA_HEADS, GLA_DV)
    y_a = o_gla.reshape(B, T, GLA_V_W).astype(x.dtype) @ w_branch_a

    aq = aq.reshape(B, T, ATT_HEADS, ATT_HEAD_DIM).astype(f32)
    ak = ak.reshape(B, T, ATT_HEADS, ATT_HEAD_DIM).astype(f32)
    av = av.reshape(B, T, ATT_HEADS, ATT_HEAD_DIM).astype(f32)
    outs, lses, new_bufs = [], [], []
    for g, (window, dil) in enumerate(ATT_GROUPS):
        hs = slice(g * ATT_HEADS_PER_GROUP, (g + 1) * ATT_HEADS_PER_GROUP)
        bias_g = rel_bias[:, hs].astype(f32)
        if win_bufs is None:
            o, lse = _dilated_prompt(aq[:, :, hs], ak[:, :, hs], av[:, :, hs], bias_g, window, dil)
            nbuf = jnp.stack([ak[:, :, hs], av[:, :, hs]], axis=2)[:, T - min(window, T):].astype(x.dtype)
        else:
            o, lse, nbuf = _dilated_sample(aq[:, :, hs], ak[:, :, hs], av[:, :, hs], win_bufs[g], bias_g, window, dil)
        outs.append(o)
        lses.append(lse)
        new_bufs.append(nbuf)
    w_grp = jax.nn.softmax(jnp.stack(lses), axis=0)
    o_att = jnp.sum(w_grp[..., None] * jnp.stack(outs), axis=0)
    y_b = o_att.reshape(B, T, ATT_OUT_W).astype(x.dtype) @ w_branch_b

    mix = (jax.nn.sigmoid(ga) * y_a + jax.nn.sigmoid(gb) * y_b) @ w_out
    x = x + gate1[:, None] * mix

    h2 = _modulate(_rmsnorm(x, norm2_g), shift2, scale2)
    if conv_state is None:
        conv_state = jnp.zeros((B, CONV_W - 1, FFN_HIDDEN), x.dtype)
    y_f, new_conv = _conv_ffn(h2, conv_state, w_up, conv_w, conv_b, w_down)
    x = x + gate2[:, None] * y_f
    return x, s_gla, new_bufs[0], new_bufs[1], new_bufs[2], new_conv


def setup_inputs(seed: int = 0) -> dict:
    key = jax.random.key(seed)
    ks = iter(jax.random.split(key, 32))

    def nrm(shape, scale):
        return jax.random.normal(next(ks), shape, jnp.float32) * scale

    in_w = int(sum(SPLIT_SIZES))
    wl = [min(w, PAST_LEN) for w, _ in ATT_GROUPS]
    return {
        'x_prompt': nrm((BATCH, SEQ, D_MODEL), 1.0),
        'x_sample': nrm((DEC_BATCH, DEC_SEQ, D_MODEL), 1.0),
        'state_gla': nrm((DEPTH, DEC_BATCH, GLA_HEADS, GLA_DK, GLA_DV), 0.5),
        'cache_win1': nrm((DEPTH, DEC_BATCH, wl[0], 2, ATT_HEADS_PER_GROUP, ATT_HEAD_DIM), 1.0),
        'cache_win2': nrm((DEPTH, DEC_BATCH, wl[1], 2, ATT_HEADS_PER_GROUP, ATT_HEAD_DIM), 1.0),
        'cache_win3': nrm((DEPTH, DEC_BATCH, wl[2], 2, ATT_HEADS_PER_GROUP, ATT_HEAD_DIM), 1.0),
        'state_conv': nrm((DEPTH, DEC_BATCH, CONV_W - 1, FFN_HIDDEN), 1.0),
        'c_prompt': nrm((BATCH, D_MODEL), 1.0),
        'c_sample': nrm((DEC_BATCH, D_MODEL), 1.0),
        'w_ada': nrm((DEPTH, D_MODEL, N_ADA * D_MODEL), 0.5 * D_MODEL ** -0.5),
        'b_ada': nrm((DEPTH, N_ADA * D_MODEL), 0.02),
        'norm1_g': 1.0 + nrm((DEPTH, D_MODEL), 0.02),
        'w_in': nrm((DEPTH, D_MODEL, in_w), D_MODEL ** -0.5),
        'gla_wg2': nrm((DEPTH, GLA_GATE_RANK, GLA_QK_W), GLA_GATE_RANK ** -0.5),
        'gla_bg': nrm((DEPTH, GLA_QK_W), 0.1),
        'gla_norm_g': 1.0 + nrm((DEPTH, GLA_DV), 0.02),
        'w_branch_a': nrm((DEPTH, GLA_V_W, D_MODEL), GLA_V_W ** -0.5),
        'w_branch_b': nrm((DEPTH, ATT_OUT_W, D_MODEL), ATT_OUT_W ** -0.5),
        'w_out': nrm((DEPTH, D_MODEL, D_MODEL), D_MODEL ** -0.5),
        'rel_bias': nrm((REL_BUCKETS, ATT_HEADS), 0.5),
        'norm2_g': 1.0 + nrm((DEPTH, D_MODEL), 0.02),
        'w_up': nrm((DEPTH, D_MODEL, 2 * FFN_HIDDEN), D_MODEL ** -0.5),
        'conv_w': nrm((DEPTH, CONV_W, FFN_HIDDEN), CONV_W ** -0.5),
        'conv_b': nrm((DEPTH, FFN_HIDDEN), 0.02),
        'w_down': nrm((DEPTH, FFN_HIDDEN, D_MODEL), FFN_HIDDEN ** -0.5),
        'normf_g': 1.0 + nrm((D_MODEL,), 0.02),
    }


def reference(x_prompt, x_sample, state_gla, cache_win1, cache_win2, cache_win3, state_conv,
              c_prompt, c_sample, w_ada, b_ada, norm1_g, w_in, gla_wg2, gla_bg, gla_norm_g,
              w_branch_a, w_branch_b, w_out, rel_bias, norm2_g, w_up, conv_w, conv_b, w_down, normf_g):
    xp, xs = x_prompt, x_sample
    sp = [[] for _ in range(5)]
    ss = [[] for _ in range(5)]
    for l in range(DEPTH):
        lw = (w_ada[l], b_ada[l], norm1_g[l], w_in[l], gla_wg2[l], gla_bg[l], gla_norm_g[l],
              w_branch_a[l], w_branch_b[l], w_out[l], norm2_g[l], w_up[l], conv_w[l], conv_b[l], w_down[l])
        xp, *new_p = _layer(xp, c_prompt, None, None, None, rel_bias, *lw)
        xs, *new_s = _layer(xs, c_sample, state_gla[l], (cache_win1[l], cache_win2[l], cache_win3[l]),
                            state_conv[l], rel_bias, *lw)
        for i in range(5):
            sp[i].append(new_p[i])
            ss[i].append(new_s[i])
    y_prompt = _rmsnorm(xp, normf_g)
    y_sample = _rmsnorm(xs, normf_g)
    gla_p, w1_p, w2_p, w3_p, conv_p = [jnp.stack(t) for t in sp]
    gla_s, w1_s, w2_s, w3_s, conv_s = [jnp.stack(t) for t in ss]
    return (y_prompt, y_sample, gla_p, gla_s, w1_p, w1_s, w2_p, w2_s, w3_p, w3_s, conv_p, conv_s)
```

```python
import functools

import numpy as np
import jax
import jax.numpy as jnp
from jax import lax
from jax.experimental import pallas as pl
from jax.experimental.pallas import tpu as pltpu

F32 = jnp.float32
BF16 = jnp.bfloat16

D_MODEL = 1024
GLA_HEADS = 4
GLA_DK = 128
GLA_DV = 256
GLA_QK_W = GLA_HEADS * GLA_DK
GLA_V_W = GLA_HEADS * GLA_DV
GLA_GATE_RANK = 16
GLA_GATE_NORM = 16.0
GLA_CHUNK = 64
ATT_GROUPS = ((128, 1), (512, 4), (2048, 16))
ATT_HPG = 8
ATT_E = 64
ATT_GW = ATT_HPG * ATT_E
ATT_BLOCK = 128
ATT_SPAN = 128
REL_BUCKETS = 32
REL_MAX_DIST = 2048
FFN_HIDDEN = 2816
N_ADA = 6
EPS = 1e-6
NEG_BIG = -1e30

LANE = 128
GLR_PAD = LANE
MAIN_W = 2 * GLA_QK_W + 2 * GLA_V_W + 2 * D_MODEL + GLR_PAD
ATT_W3 = 3 * 3 * ATT_GW
COL_CHUNK = 512
VMEM_LIMIT = 56 * 1024 * 1024

_NT = (((1,), (1,)), ((), ()))
_TN = (((0,), (0,)), ((), ()))


def _params(n_axes):
    return pltpu.CompilerParams(dimension_semantics=("arbitrary",) * n_axes,
                                vmem_limit_bytes=VMEM_LIMIT)


def _resident(shape):
    zeros = (0,) * len(shape)
    return pl.BlockSpec(shape, lambda *_: zeros, pipeline_mode=pl.Buffered(1))


def _silu(x):
    return x * jax.nn.sigmoid(x)


def _rms(x):
    return x * lax.rsqrt(jnp.mean(x * x, axis=-1, keepdims=True) + EPS)


def _split3(x):
    a = x.astype(BF16)
    r = x - a.astype(F32)
    b = r.astype(BF16)
    c = (r - b.astype(F32)).astype(BF16)
    return a, b, c


def _dot(a, b):
    return jnp.dot(a, b, preferred_element_type=F32)


def _dot_exact_lhs(x, m):
    a, b, c = _split3(x)
    return _dot(a, m) + _dot(b, m) + _dot(c, m)


def _ada_kernel(c_ref, w_ref, b_ref, o_ref):
    s = _silu(c_ref[...]).astype(BF16)
    o_ref[...] = _dot(s, w_ref[...]) + b_ref[...]


def _ada(c_all, w_ada, b_ada):
    m = c_all.shape[0]
    n = w_ada.shape[1]
    tn = D_MODEL
    return pl.pallas_call(
        _ada_kernel,
        grid=(n // tn,),
        in_specs=[pl.BlockSpec((m, D_MODEL), lambda j: (0, 0)),
                  pl.BlockSpec((D_MODEL, tn), lambda j: (0, j)),
                  pl.BlockSpec((1, tn), lambda j: (0, j))],
        out_specs=pl.BlockSpec((m, tn), lambda j: (0, j)),
        out_shape=jax.ShapeDtypeStruct((m, n), F32),
        compiler_params=_params(1),
        name="ada",
    )(c_all, w_ada, b_ada)


def _ada_spec(ada, tm, k):
    if ada.shape[1] == 1:
        return pl.BlockSpec((1, 1, D_MODEL), lambda b, i: (b, 0, k))
    return pl.BlockSpec((1, tm, D_MODEL), lambda b, i: (b, i, k))


def _inproj_kernel(x_ref, sh_ref, sc_ref, g_ref, w_ref, pm_ref, pa_ref, c1_ref, c2_ref, c3_ref):
    tm = x_ref.shape[1]
    h = _rms(x_ref[0]) * g_ref[...]
    h = h * (1.0 + sc_ref[0]) + sh_ref[0]
    hb = h.astype(BF16)
    for c0 in range(0, MAIN_W, COL_CHUNK):
        c1 = min(c0 + COL_CHUNK, MAIN_W)
        pm_ref[0, :, c0:c1] = _dot(hb, w_ref[:, c0:c1]).astype(BF16)
    caches = (c1_ref, c2_ref, c3_ref)
    for j in range(ATT_W3 // ATT_GW):
        acc = _dot(hb, w_ref[:, MAIN_W + j * ATT_GW:MAIN_W + (j + 1) * ATT_GW])
        pa_ref[0, :, j * ATT_GW:(j + 1) * ATT_GW] = acc.astype(BF16)
        g, part = divmod(j, 3)
        if part:
            rows = caches[g].shape[1]
            caches[g][0, :, (part - 1) * ATT_GW:part * ATT_GW] = acc[tm - rows:, :]


def _inproj(x, ada, norm_g, w_all, tm, windows):
    bsz, t, _ = x.shape
    nt = t // tm
    cache_specs, cache_shapes = [], []
    for w in windows:
        rows = min(tm, w)
        nblk = w // rows
        cache_specs.append(pl.BlockSpec(
            (1, rows, 2 * ATT_GW),
            functools.partial(lambda b, i, nblk: (b, jnp.maximum(i - (nt - nblk), 0), 0), nblk=nblk)))
        cache_shapes.append(jax.ShapeDtypeStruct((bsz, w, 2 * ATT_GW), F32))
    return pl.pallas_call(
        _inproj_kernel,
        grid=(bsz, nt),
        in_specs=[pl.BlockSpec((1, tm, D_MODEL), lambda b, i: (b, i, 0)),
                  _ada_spec(ada, tm, 0), _ada_spec(ada, tm, 1),
                  _resident((1, D_MODEL)), _resident(w_all.shape)],
        out_specs=[pl.BlockSpec((1, tm, MAIN_W), lambda b, i: (b, i, 0)),
                   pl.BlockSpec((1, tm, ATT_W3), lambda b, i: (b, i, 0))] + cache_specs,
        out_shape=[jax.ShapeDtypeStruct((bsz, t, MAIN_W), BF16),
                   jax.ShapeDtypeStruct((bsz, t, ATT_W3), BF16)] + cache_shapes,
        compiler_params=_params(2),
        name="inproj",
    )(x, ada, ada, norm_g, w_all)


def _log_sigmoid(x):
    return jnp.minimum(x, 0.0) - jnp.log1p(jnp.exp(-jnp.abs(x)))


def _gla_kernel(qk_ref, v_ref, gr_ref, glr_ref, wg2_ref, bg_ref, gn_ref, o_ref, s_ref, st_ref):
    t = pl.program_id(1)
    c = GLA_CHUNK
    n_chunks = qk_ref.shape[1] // c

    @pl.when(t == 0)
    def _():
        st_ref[...] = jnp.zeros_like(st_ref)

    row = lax.broadcasted_iota(jnp.int32, (c, c), 0)
    col = lax.broadcasted_iota(jnp.int32, (c, c), 1)
    tril = row >= col
    tri_b = jnp.where(tril, 1.0, 0.0).astype(BF16)

    def chunk(ci, carry):
        r0 = pl.multiple_of(ci * c, c)
        rows = pl.ds(r0, c)
        logits = _dot(glr_ref[0, rows, :], wg2_ref[...]) + bg_ref[...]
        g = _log_sigmoid(logits) * (1.0 / GLA_GATE_NORM)
        b = _dot_exact_lhs_t(tri_b, g)
        bref = b[c // 2:c // 2 + 1, :]
        blast = b[c - 1:c, :]
        e_in = jnp.exp(b - bref)
        e_out = jnp.exp(bref - b)
        e_b = jnp.exp(b)
        e_k = jnp.exp(blast - b)
        e_last = jnp.exp(blast)
        qk = qk_ref[0, rows, :].astype(F32)
        q = qk[:, :GLA_QK_W] * (GLA_DK ** -0.5)
        k = qk[:, GLA_QK_W:]
        v = v_ref[0, rows, :]
        gr = gr_ref[0, rows, :].astype(F32)
        for h in range(GLA_HEADS):
            ks = slice(h * GLA_DK, (h + 1) * GLA_DK)
            vs = slice(h * GLA_DV, (h + 1) * GLA_DV)
            qh, kh, vh = q[:, ks], k[:, ks], v[:, vs]
            a = lax.dot_general((qh * e_in[:, ks]).astype(BF16), (kh * e_out[:, ks]).astype(BF16),
                                _NT, preferred_element_type=F32)
            a = jnp.where(tril, a, 0.0)
            st = st_ref[h]
            o = _dot(a.astype(BF16), vh) + lax.dot_general(
                (qh * e_b[:, ks]).astype(BF16), st.astype(BF16), _NT, preferred_element_type=F32)
            kd = (kh * e_k[:, ks]).astype(BF16)
            st_ref[h] = st * e_last[:, ks] + lax.dot_general(vh, kd, _TN, preferred_element_type=F32)
            grh = gr[:, vs]
            o_ref[0, rows, vs] = (_rms(o) * gn_ref[...] * _silu(grh)).astype(BF16)
        return carry

    lax.fori_loop(0, n_chunks, chunk, 0)

    @pl.when(t == pl.num_programs(1) - 1)
    def _():
        for h in range(GLA_HEADS):
            s_ref[0, h] = st_ref[h].T


def _dot_exact_lhs_t(m, x):
    a, b, c = _split3(x)
    return _dot(m, a) + _dot(m, b) + _dot(m, c)


def _gla_prompt(pm, wg2p, bg, gn, tc):
    bsz, t, _ = pm.shape
    blk = lambda k: pl.BlockSpec((1, tc, D_MODEL), functools.partial(lambda b, i, k: (b, i, k), k=k))
    return pl.pallas_call(
        _gla_kernel,
        grid=(bsz, t // tc),
        in_specs=[blk(0), blk(1), blk(2),
                  pl.BlockSpec((1, tc, GLR_PAD), lambda b, i: (b, i, (MAIN_W - GLR_PAD) // GLR_PAD)),
                  _resident(wg2p.shape), _resident(bg.shape), _resident(gn.shape)],
        out_specs=[pl.BlockSpec((1, tc, GLA_V_W), lambda b, i: (b, i, 0)),
                   pl.BlockSpec((1, GLA_HEADS, GLA_DK, GLA_DV), lambda b, i: (b, 0, 0, 0))],
        out_shape=[jax.ShapeDtypeStruct((bsz, t, GLA_V_W), BF16),
                   jax.ShapeDtypeStruct((bsz, GLA_HEADS, GLA_DK, GLA_DV), F32)],
        scratch_shapes=[pltpu.VMEM((GLA_HEADS, GLA_DV, GLA_DK), F32)],
        compiler_params=_params(2),
        name="gla_prompt",
    )(pm, pm, pm, pm, wg2p, bg, gn)


def _to_col(row):
    n = row.shape[1]
    eye = lax.broadcasted_iota(jnp.int32, (n, n), 0) == lax.broadcasted_iota(jnp.int32, (n, n), 1)
    return jnp.sum(jnp.where(eye, jnp.broadcast_to(row, (n, n)), 0.0), axis=1, keepdims=True)


def _gla_step_kernel(pm_ref, s0_ref, wg2_ref, bg_ref, gn_ref, o_ref, s_ref):
    p = pm_ref[0]
    glr = jnp.broadcast_to(p[:, MAIN_W - GLR_PAD:], (8, GLR_PAD))
    logits = _dot(glr, wg2_ref[...])[0:1] + bg_ref[...]
    eg = jnp.exp(_log_sigmoid(logits) * (1.0 / GLA_GATE_NORM))
    pf = p.astype(F32)
    q = pf[:, :GLA_QK_W] * (GLA_DK ** -0.5)
    k = pf[:, GLA_QK_W:2 * GLA_QK_W]
    v = pf[:, 2 * GLA_QK_W:2 * GLA_QK_W + GLA_V_W]
    gr = pf[:, 2 * GLA_QK_W + GLA_V_W:2 * GLA_QK_W + 2 * GLA_V_W]
    for h in range(GLA_HEADS):
        ks = slice(h * GLA_DK, (h + 1) * GLA_DK)
        vs = slice(h * GLA_DV, (h + 1) * GLA_DV)
        s_new = s0_ref[0, h] * _to_col(eg[:, ks]) + _to_col(k[:, ks]) * v[:, vs]
        s_ref[0, h] = s_new
        o = jnp.sum(_to_col(q[:, ks]) * s_new, axis=0, keepdims=True)
        o_ref[0, :, vs] = (_rms(o) * gn_ref[...] * _silu(gr[:, vs])).astype(BF16)


def _gla_sample(pm, s0, wg2p, bg, gn):
    n = pm.shape[0]
    st_spec = pl.BlockSpec((1, GLA_HEADS, GLA_DK, GLA_DV), lambda b: (b, 0, 0, 0))
    return pl.pallas_call(
        _gla_step_kernel,
        grid=(n,),
        in_specs=[pl.BlockSpec((1, 1, MAIN_W), lambda b: (b, 0, 0)), st_spec,
                  _resident(wg2p.shape), _resident(bg.shape), _resident(gn.shape)],
        out_specs=[pl.BlockSpec((1, 1, GLA_V_W), lambda b: (b, 0, 0)), st_spec],
        out_shape=[jax.ShapeDtypeStruct((n, 1, GLA_V_W), BF16),
                   jax.ShapeDtypeStruct(s0.shape, F32)],
        compiler_params=_params(1),
        name="gla_sample",
    )(pm, s0, wg2p, bg, gn)


def _att_kernel(q_ref, kp_ref, kc_ref, vp_ref, vc_ref, bias_ref, o_ref, lse_ref):
    first = jnp.where(pl.program_id(2) == 0, 1, 0)
    q = q_ref[0]
    kk = jnp.concatenate([kp_ref[0], kc_ref[0]], axis=0)
    vv = jnp.concatenate([vp_ref[0], vc_ref[0]], axis=0)
    lane = lax.broadcasted_iota(jnp.int32, (ATT_BLOCK, LANE), 1)
    low = lane < ATT_E
    half = (jnp.where(low, 1.0, 0.0).astype(BF16), jnp.where(low, 0.0, 1.0).astype(BF16))
    lse_tile = jnp.zeros((ATT_BLOCK, LANE), F32)
    for p in range(ATT_HPG // 2):
        cs = slice(p * LANE, (p + 1) * LANE)
        qp, kp, vp = q[:, cs], kk[:, cs], vv[:, cs]
        outs = []
        for hh in range(2):
            h = 2 * p + hh
            s = lax.dot_general(qp * half[hh], kp, _NT, preferred_element_type=F32)
            s = s * (ATT_E ** -0.5) + bias_ref[first, h]
            m = jnp.max(s, axis=-1, keepdims=True)
            e = jnp.exp(s - m)
            den = jnp.sum(e, axis=-1, keepdims=True)
            outs.append(_dot(e.astype(BF16), vp) / den)
            lse_tile = jnp.where(lane == h, m + jnp.log(den), lse_tile)
        o_ref[0, :, cs] = jnp.where(low, outs[0], outs[1]).astype(BF16)
    lse_ref[0] = lse_tile


def _att_prompt(pa, bias2, g, dil):
    bsz, t, _ = pa.shape
    l = t // dil
    nb = l // ATT_BLOCK
    nblk_cols = ATT_W3 // ATT_GW
    pav = pa.reshape(bsz, l, dil * ATT_W3)

    def spec(part, prev):
        def imap(b, r, n):
            return (b, jnp.maximum(n - 1, 0) if prev else n, r * nblk_cols + 3 * g + part)
        return pl.BlockSpec((1, ATT_BLOCK, ATT_GW), imap)

    o, lse = pl.pallas_call(
        _att_kernel,
        grid=(bsz, dil, nb),
        in_specs=[spec(0, False), spec(1, True), spec(1, False), spec(2, True), spec(2, False),
                  _resident(bias2.shape)],
        out_specs=[pl.BlockSpec((1, ATT_BLOCK, ATT_GW), lambda b, r, n: (b, n, r)),
                   pl.BlockSpec((1, ATT_BLOCK, LANE), lambda b, r, n: (b, n, r))],
        out_shape=[jax.ShapeDtypeStruct((bsz, l, dil * ATT_GW), BF16),
                   jax.ShapeDtypeStruct((bsz, l, dil * LANE), F32)],
        compiler_params=_params(3),
        name=f"att_prompt_g{g}",
    )(pav, pav, pav, pav, pav, bias2)
    return o.reshape(bsz, t, ATT_GW), lse.reshape(bsz, t, LANE)


def _att_step_kernel(q_ref, n1_ref, n2_ref, n3_ref, b1_ref, b2_ref, b3_ref, sb_ref, sb0_ref, o_ref, l_ref):
    news = (n1_ref, n2_ref, n3_ref)
    bufs = (b1_ref, b2_ref, b3_ref)
    scale = ATT_E ** -0.5
    for g in range(len(ATT_GROUPS)):
        q = q_ref[g]
        kb, vb = bufs[g][:, 0], bufs[g][:, 1]
        kn, vn = news[g][0], news[g][1]
        s = jnp.sum(kb * q, axis=-1, keepdims=True) * scale + sb_ref[g]
        sn = jnp.sum(kn * q, axis=-1, keepdims=True) * scale + sb0_ref[g]
        m = jnp.maximum(jnp.max(s, axis=0), sn)
        e = jnp.exp(s - m)
        en = jnp.exp(sn - m)
        den = jnp.sum(e, axis=0) + en
        o_ref[g] = (jnp.sum(e * vb, axis=0) + en * vn) / den
        l_ref[g] = jnp.broadcast_to(m + jnp.log(den), (ATT_HPG, LANE))


def _att_sample(q, news, caches, sbias, sbias0):
    n = q.shape[0]
    bufs, buf_specs = [], []
    for (w, dil), c in zip(ATT_GROUPS, caches):
        bufs.append(c.reshape(n, w // dil, dil, 2, ATT_HPG, ATT_E))
        buf_specs.append(pl.BlockSpec((None, ATT_SPAN, None, 2, ATT_HPG, ATT_E),
                                      lambda b: (b, 0, 0, 0, 0, 0)))
    new_spec = pl.BlockSpec((None, 2, ATT_HPG, ATT_E), lambda b: (b, 0, 0, 0))
    per_seq = lambda width: pl.BlockSpec((None, 3, ATT_HPG, width), lambda b: (b, 0, 0, 0))
    return pl.pallas_call(
        _att_step_kernel,
        grid=(n,),
        in_specs=[per_seq(ATT_E)] + [new_spec] * 3 + buf_specs
                 + [_resident(sbias.shape), _resident(sbias0.shape)],
        out_specs=[per_seq(ATT_E), per_seq(LANE)],
        out_shape=[jax.ShapeDtypeStruct((n, 3, ATT_HPG, ATT_E), F32),
                   jax.ShapeDtypeStruct((n, 3, ATT_HPG, LANE), F32)],
        compiler_params=_params(1),
        name="att_sample",
    )(q, *news, *bufs, sbias, sbias0)


CACHE_DMA_SPLIT = 4


def _cache_shift_copies(bufs, news, outs, sem):
    copies = []
    for buf, new, out in zip(bufs, news, outs):
        _, n, lb = buf.shape[:3]
        step = n // CACHE_DMA_SPLIT
        for j in range(CACHE_DMA_SPLIT):
            bs = pl.ds(j * step, step)
            copies.append(pltpu.make_async_copy(
                buf.at[0, bs, pl.ds(1, lb - 1)], out.at[0, bs, pl.ds(0, lb - 1)], sem.at[len(copies)]))
        copies.append(pltpu.make_async_copy(new, out.at[0, :, pl.ds(lb - 1, 1)], sem.at[len(copies)]))
    return copies


def _cache_shift_kernel(b1, b2, b3, n1, n2, n3, o1, o2, o3, sem):
    copies = _cache_shift_copies((b1, b2, b3), (n1, n2, n3), (o1, o2, o3), sem)
    for cp in copies:
        cp.start()
    for cp in copies:
        cp.wait()


def _cache_shift(caches, news):
    anyspec = pl.BlockSpec(memory_space=pl.ANY)
    return pl.pallas_call(
        _cache_shift_kernel,
        in_specs=[anyspec] * 6,
        out_specs=[anyspec] * 3,
        out_shape=[jax.ShapeDtypeStruct(c.shape, c.dtype) for c in caches],
        scratch_shapes=[pltpu.SemaphoreType.DMA((3 * (CACHE_DMA_SPLIT + 1),))],
        name="cache_shift",
    )(*caches, *news)


def _mix_ffn_kernel(*refs, per_row_conv):
    (x_ref, oa_ref, o1_ref, o2_ref, o3_ref, l1_ref, l2_ref, l3_ref, ga_ref, gb_ref,
     g1_ref, sh2_ref, sc2_ref, g2_ref, n2_ref, nf_ref, et_ref,
     wa_ref, wb_ref, wo_ref, wup_ref, cw_ref, cb_ref, wdn_ref) = refs[:24]
    if per_row_conv:
        cs0_ref, cs1_ref, y_ref, cv_ref = refs[24:]
    else:
        y_ref, cv_ref, carry_ref = refs[24:]
    tm = x_ref.shape[1]

    ls = (l1_ref[0], l2_ref[0], l3_ref[0])
    m = jnp.maximum(jnp.maximum(ls[0], ls[1]), ls[2])
    ws = [jnp.exp(l - m) for l in ls]
    den = ws[0] + ws[1] + ws[2]
    o_att = jnp.zeros((tm, ATT_GW), F32)
    for w, o_ref in zip(ws, (o1_ref, o2_ref, o3_ref)):
        o_att = o_att + _dot_exact_lhs(w / den, et_ref[...]) * o_ref[0].astype(F32)

    y_a = _dot(oa_ref[0], wa_ref[...])
    y_b = _dot(o_att.astype(BF16), wb_ref[...])
    mix_in = jax.nn.sigmoid(ga_ref[0].astype(F32)) * y_a + jax.nn.sigmoid(gb_ref[0].astype(F32)) * y_b
    x1 = x_ref[0] + g1_ref[0] * _dot(mix_in.astype(BF16), wo_ref[...])

    h2 = (_rms(x1) * n2_ref[...] * (1.0 + sc2_ref[0]) + sh2_ref[0]).astype(BF16)
    a = _dot(h2, wup_ref[:, :FFN_HIDDEN])
    bgate = _dot(h2, wup_ref[:, FFN_HIDDEN:])
    if per_row_conv:
        a2, a1 = cs0_ref[0], cs1_ref[0]
        cv_ref[0] = a
    else:
        @pl.when(pl.program_id(1) == 0)
        def _():
            carry_ref[...] = jnp.zeros_like(carry_ref)
        row = lax.broadcasted_iota(jnp.int32, (tm, FFN_HIDDEN), 0)
        prev = carry_ref[...]
        a1 = jnp.where(row == 0, prev[7:8], pltpu.roll(a, 1, 0))
        a2 = jnp.where(row == 0, prev[6:7], jnp.where(row == 1, prev[7:8], pltpu.roll(a, 2, 0)))
        carry_ref[...] = a[tm - 8:, :]
        cv_ref[0] = a[tm - 8:, :]
    a_conv = cb_ref[...] + (cw_ref[0:1] * a2 + cw_ref[1:2] * a1 + cw_ref[2:3] * a)
    y_f = _dot((_silu(a_conv) * bgate).astype(BF16), wdn_ref[...])
    x2 = x1 + g2_ref[0] * y_f
    y_ref[0] = _rms(x2) * nf_ref[...]


def _mix_ffn(x, oa, os_, ls_, pm, ada, n2g, nfg, emat_t, wa, wb, wo, wup, cw, cb, wdn, tm, conv_state=None):
    bsz, t, _ = x.shape
    per_row = conv_state is not None
    rows = lambda width, k=0: pl.BlockSpec(
        (1, tm, width), functools.partial(lambda b, i, k: (b, i, k), k=k))
    in_specs = ([rows(D_MODEL), rows(GLA_V_W)] + [rows(ATT_GW)] * 3 + [rows(LANE)] * 3
                + [rows(D_MODEL, 3), rows(D_MODEL, 4)]
                + [_ada_spec(ada, tm, k) for k in (2, 3, 4, 5)]
                + [_resident(a.shape) for a in (n2g, nfg, emat_t, wa, wb, wo, wup, cw, cb, wdn)])
    args = [x, oa, *os_, *ls_, pm, pm, ada, ada, ada, ada, n2g, nfg, emat_t, wa, wb, wo, wup, cw, cb, wdn]
    scratch = []
    if per_row:
        in_specs += [rows(FFN_HIDDEN)] * 2
        args += [conv_state[:, 0][None], conv_state[:, 1][None]]
        cv_spec = rows(FFN_HIDDEN)
        cv_shape = jax.ShapeDtypeStruct((bsz, t, FFN_HIDDEN), F32)
    else:
        cv_spec = pl.BlockSpec((1, 8, FFN_HIDDEN), lambda b, i: (b, 0, 0))
        cv_shape = jax.ShapeDtypeStruct((bsz, 8, FFN_HIDDEN), F32)
        scratch = [pltpu.VMEM((8, FFN_HIDDEN), F32)]
    return pl.pallas_call(
        functools.partial(_mix_ffn_kernel, per_row_conv=per_row),
        grid=(bsz, t // tm),
        in_specs=in_specs,
        out_specs=[rows(D_MODEL), cv_spec],
        out_shape=[jax.ShapeDtypeStruct((bsz, t, D_MODEL), F32), cv_shape],
        scratch_shapes=scratch,
        compiler_params=_params(2),
        name="mix_ffn_sample" if per_row else "mix_ffn_prompt",
    )(*args)


def _t5_bucket(dist):
    max_exact = REL_BUCKETS // 2
    d = np.maximum(dist, 1).astype(np.float32)
    large = max_exact + (np.log(d / max_exact) / np.log(REL_MAX_DIST / max_exact)
                         * (REL_BUCKETS - max_exact)).astype(np.int32)
    large = np.minimum(large, REL_BUCKETS - 1)
    return np.where(dist < max_exact, dist, large).astype(np.int32)


def _prompt_bias(rel_bias_g, dil):
    i = np.arange(ATT_BLOCK)[:, None]
    j = np.arange(2 * ATT_BLOCK)[None, :]
    off = ATT_BLOCK + i - j
    valid = (off >= 0) & (off <= ATT_SPAN)
    bucket = _t5_bucket(np.clip(off, 0, ATT_SPAN) * dil)
    bias = jnp.transpose(rel_bias_g[bucket], (2, 0, 1))
    normal = jnp.where(valid[None], bias, NEG_BIG)
    first = jnp.where((valid & (j >= ATT_BLOCK))[None], bias, NEG_BIG)
    return jnp.stack([normal, first])


def _sample_bias(rel_bias):
    sb, sb0 = [], []
    for g, (_, dil) in enumerate(ATT_GROUPS):
        rb = rel_bias[:, g * ATT_HPG:(g + 1) * ATT_HPG]
        m = ATT_SPAN - np.arange(ATT_SPAN)
        sb.append(rb[_t5_bucket(m * dil)][..., None])
        sb0.append(rb[_t5_bucket(np.zeros((), np.int64))][..., None])
    return jnp.stack(sb), jnp.stack(sb0)


def _head_indicator_t():
    e = (np.arange(LANE)[:, None] == np.arange(ATT_GW)[None, :] // ATT_E).astype(np.float32)
    return jnp.asarray(e, BF16)


def _permute_w_in(w_in):
    sizes = (GLA_QK_W, GLA_QK_W, GLA_V_W, GLA_V_W, GLA_GATE_RANK, 3 * ATT_GW, 3 * ATT_GW, 3 * ATT_GW,
             D_MODEL, D_MODEL)
    offs = np.concatenate([[0], np.cumsum(sizes)])
    seg = lambda i: w_in[:, offs[i]:offs[i + 1]]
    gq, gk, gv, gr, glr, aq, ak, av, ga, gb = [seg(i) for i in range(10)]
    glr = jnp.pad(glr, ((0, 0), (0, GLR_PAD - GLA_GATE_RANK)))
    att = []
    for g in range(len(ATT_GROUPS)):
        cs = slice(g * ATT_GW, (g + 1) * ATT_GW)
        att += [aq[:, cs], ak[:, cs], av[:, cs]]
    return jnp.concatenate([gq, gk, gv, gr, ga, gb, glr] + att, axis=1).astype(BF16)


PROMPT_TM = 256
SAMPLE_TM = 128
GLA_TC = 512


def kernel(x_prompt, x_sample, state_gla, cache_win1, cache_win2, cache_win3, state_conv, c_prompt, c_sample,
           w_ada, b_ada, norm1_g, w_in, gla_wg2, gla_bg, gla_norm_g, w_branch_a, w_branch_b, w_out, rel_bias,
           norm2_g, w_up, conv_w, conv_b, w_down, normf_g):
    bp, tp, _ = x_prompt.shape
    ns = x_sample.shape[0]
    assert w_ada.shape[0] == 1, "single-layer trunk"

    w_all = _permute_w_in(w_in[0])
    wg2p = jnp.pad(gla_wg2[0], ((0, GLR_PAD - GLA_GATE_RANK), (0, 0))).astype(BF16)
    bg = gla_bg[0][None]
    gn = gla_norm_g[0][None]
    n1g, n2g, nfg = norm1_g[0][None], norm2_g[0][None], normf_g[None]
    wa, wb, wo = w_branch_a[0].astype(BF16), w_branch_b[0].astype(BF16), w_out[0].astype(BF16)
    wup, wdn = w_up[0].astype(BF16), w_down[0].astype(BF16)
    cw, cb = conv_w[0], conv_b[0][None]
    emat_t = _head_indicator_t()
    sbias, sbias0 = _sample_bias(rel_bias)

    ada = _ada(jnp.concatenate([c_prompt, c_sample], axis=0), w_ada[0].astype(BF16), b_ada[0][None])
    ada_p = ada[:bp][:, None, :]
    ada_s = ada[bp:][None]
    xs = x_sample.reshape(1, ns, D_MODEL)

    pm_p, pa_p, w1_p, w2_p, w3_p = _inproj(x_prompt, ada_p, n1g, w_all, PROMPT_TM,
                                           tuple(min(w, tp) for w, _ in ATT_GROUPS))
    oa_p, gla_p = _gla_prompt(pm_p, wg2p, bg, gn, GLA_TC)
    os_p, ls_p = [], []
    for g, (_, dil) in enumerate(ATT_GROUPS):
        bias2 = _prompt_bias(rel_bias[:, g * ATT_HPG:(g + 1) * ATT_HPG], dil)
        o, l = _att_prompt(pa_p, bias2, g, dil)
        os_p.append(o)
        ls_p.append(l)
    y_p, cv_p = _mix_ffn(x_prompt, oa_p, os_p, ls_p, pm_p, ada_p, n2g, nfg, emat_t,
                         wa, wb, wo, wup, cw, cb, wdn, PROMPT_TM)

    pm_s, pa_s, n1, n2, n3 = _inproj(xs, ada_s, n1g, w_all, SAMPLE_TM, (ns, ns, ns))
    news = [n.reshape(ns, 2, ATT_HPG, ATT_E) for n in (n1, n2, n3)]
    caches = (cache_win1, cache_win2, cache_win3)
    oa_s, gla_s = _gla_sample(pm_s.reshape(ns, 1, MAIN_W), state_gla[0], wg2p, bg, gn)
    q_s = pa_s.reshape(ns, 3, 3, ATT_HPG, ATT_E)[:, :, 0].astype(F32)
    o_s, l_s = _att_sample(q_s, news, caches, sbias, sbias0)
    os_s = [o_s[:, g].reshape(1, ns, ATT_GW).astype(BF16) for g in range(3)]
    ls_s = [jnp.pad(l_s[:, g, :, 0], ((0, 0), (0, LANE - ATT_HPG)))[None] for g in range(3)]
    new_caches = _cache_shift(caches, [n[:, None] for n in news])
    y_s, cv_s = _mix_ffn(xs, oa_s.reshape(1, ns, GLA_V_W), os_s, ls_s, pm_s, ada_s, n2g, nfg, emat_t,
                         wa, wb, wo, wup, cw, cb, wdn, SAMPLE_TM, conv_state=state_conv[0])

    kv = lambda c: c.reshape(1, bp, c.shape[1], 2, ATT_HPG, ATT_E)
    conv_s = jnp.stack([state_conv[0][:, 1], cv_s[0]], axis=1)[None]
    return (y_p, y_s.reshape(ns, 1, D_MODEL), gla_p[None], gla_s[None],
            kv(w1_p), new_caches[0], kv(w2_p), new_caches[1], kv(w3_p), new_caches[2],
            cv_p[:, 6:8][None], conv_s)
```

```python
import functools

import numpy as np
import jax
import jax.numpy as jnp
from jax import lax
from jax.experimental import pallas as pl
from jax.experimental.pallas import tpu as pltpu

F32 = jnp.float32
BF16 = jnp.bfloat16

D_MODEL = 1024
GLA_HEADS = 4
GLA_DK = 128
GLA_DV = 256
GLA_QK_W = GLA_HEADS * GLA_DK
GLA_V_W = GLA_HEADS * GLA_DV
GLA_GATE_RANK = 16
GLA_GATE_NORM = 16.0
GLA_CHUNK = 64
ATT_GROUPS = ((128, 1), (512, 4), (2048, 16))
ATT_HPG = 8
ATT_E = 64
ATT_GW = ATT_HPG * ATT_E
ATT_BLOCK = 128
ATT_SPAN = 128
REL_BUCKETS = 32
REL_MAX_DIST = 2048
FFN_HIDDEN = 2816
N_ADA = 6
EPS = 1e-6
NEG_BIG = -1e30

LANE = 128
GLR_PAD = LANE
MAIN_W = 2 * GLA_QK_W + 2 * GLA_V_W + 2 * D_MODEL + GLR_PAD
ATT_W3 = 3 * 3 * ATT_GW
COL_CHUNK = 512
VMEM_LIMIT = 56 * 1024 * 1024

_NT = (((1,), (1,)), ((), ()))
_TN = (((0,), (0,)), ((), ()))


def _params(n_axes):
    return pltpu.CompilerParams(dimension_semantics=("arbitrary",) * n_axes,
                                vmem_limit_bytes=VMEM_LIMIT)


def _resident(shape):
    zeros = (0,) * len(shape)
    return pl.BlockSpec(shape, lambda *_: zeros, pipeline_mode=pl.Buffered(1))


def _silu(x):
    return x * jax.nn.sigmoid(x)


def _rms(x):
    return x * lax.rsqrt(jnp.mean(x * x, axis=-1, keepdims=True) + EPS)


def _split3(x):
    a = x.astype(BF16)
    r = x - a.astype(F32)
    b = r.astype(BF16)
    c = (r - b.astype(F32)).astype(BF16)
    return a, b, c


def _dot(a, b):
    return jnp.dot(a, b, preferred_element_type=F32)


def _dot_exact_lhs(x, m):
    a, b, c = _split3(x)
    return _dot(a, m) + _dot(b, m) + _dot(c, m)


def _ada_kernel(c_ref, w_ref, b_ref, o_ref):
    s = _silu(c_ref[...]).astype(BF16)
    o_ref[...] = _dot(s, w_ref[...]) + b_ref[...]


def _ada(c_all, w_ada, b_ada):
    m = c_all.shape[0]
    n = w_ada.shape[1]
    tn = D_MODEL
    return pl.pallas_call(
        _ada_kernel,
        grid=(n // tn,),
        in_specs=[pl.BlockSpec((m, D_MODEL), lambda j: (0, 0)),
                  pl.BlockSpec((D_MODEL, tn), lambda j: (0, j)),
                  pl.BlockSpec((1, tn), lambda j: (0, j))],
        out_specs=pl.BlockSpec((m, tn), lambda j: (0, j)),
        out_shape=jax.ShapeDtypeStruct((m, n), F32),
        compiler_params=_params(1),
        name="ada",
    )(c_all, w_ada, b_ada)


def _ada_spec(ada, tm, k):
    if ada.shape[1] == 1:
        return pl.BlockSpec((1, 1, D_MODEL), lambda b, i: (b, 0, k))
    return pl.BlockSpec((1, tm, D_MODEL), lambda b, i: (b, i, k))


def _inproj_kernel(x_ref, sh_ref, sc_ref, g_ref, w_ref, pm_ref, pa1_ref, pa2_ref, pa3_ref,
                   c1_ref, c2_ref, c3_ref, stage_ref, *, first_cache_step):
    tm = x_ref.shape[1]
    i = pl.program_id(1)
    h = _rms(x_ref[0]) * g_ref[...]
    h = h * (1.0 + sc_ref[0]) + sh_ref[0]
    hb = h.astype(BF16)
    for c0 in range(0, MAIN_W, COL_CHUNK):
        c1 = min(c0 + COL_CHUNK, MAIN_W)
        pm_ref[0, :, c0:c1] = _dot(hb, w_ref[:, c0:c1]).astype(BF16)
    for g, (pa_ref, cache_ref) in enumerate(zip((pa1_ref, pa2_ref, pa3_ref), (c1_ref, c2_ref, c3_ref))):
        dil = pa_ref.shape[0]
        rows = cache_ref.shape[-1]
        for part in range(3):
            cs = slice(part * ATT_GW, (part + 1) * ATT_GW)
            c0 = MAIN_W + (3 * g + part) * ATT_GW
            acc = _dot(hb, w_ref[:, c0:c0 + ATT_GW])
            if dil == 1:
                pa_ref[0, :, cs] = acc.astype(BF16)
            else:
                for c in range(ATT_GW // LANE):
                    stage_ref[c] = acc[:, c * LANE:(c + 1) * LANE]
                for r in range(dil):
                    for c in range(ATT_GW // LANE):
                        c0 = part * ATT_GW + c * LANE
                        pa_ref[r, :, c0:c0 + LANE] = stage_ref[
                            c, pl.ds(r, tm // dil, stride=dil), :].astype(BF16)
            if part:
                @pl.when(i >= first_cache_step[g])
                def _(acc=acc, cache_ref=cache_ref, part=part, rows=rows):
                    tail = acc[tm - rows:, :].T
                    cache_ref[part - 1] = tail.reshape(ATT_HPG, ATT_E, rows)


def _inproj(x, ada, norm_g, w_all, tm, windows, dils):
    bsz, t, _ = x.shape
    nt = t // tm
    pa_specs, pa_shapes, cache_specs, cache_shapes, first_steps = [], [], [], [], []
    for w, dil in zip(windows, dils):
        rows = min(tm, w)
        nblk = w // rows
        first_steps.append(nt - nblk)
        pa_specs.append(pl.BlockSpec((None, dil, tm // dil, 3 * ATT_GW), lambda b, i: (b, 0, i, 0)))
        pa_shapes.append(jax.ShapeDtypeStruct((bsz, dil, t // dil, 3 * ATT_GW), BF16))
        cache_specs.append(pl.BlockSpec(
            (None, 2, ATT_HPG, ATT_E, rows),
            functools.partial(lambda b, i, first: (b, 0, 0, 0, jnp.maximum(i - first, 0)), first=nt - nblk)))
        cache_shapes.append(jax.ShapeDtypeStruct((bsz, 2, ATT_HPG, ATT_E, w), F32))
    return pl.pallas_call(
        functools.partial(_inproj_kernel, first_cache_step=tuple(first_steps)),
        grid=(bsz, nt),
        in_specs=[pl.BlockSpec((1, tm, D_MODEL), lambda b, i: (b, i, 0)),
                  _ada_spec(ada, tm, 0), _ada_spec(ada, tm, 1),
                  _resident((1, D_MODEL)), _resident(w_all.shape)],
        out_specs=[pl.BlockSpec((1, tm, MAIN_W), lambda b, i: (b, i, 0))] + pa_specs + cache_specs,
        out_shape=[jax.ShapeDtypeStruct((bsz, t, MAIN_W), BF16)] + pa_shapes + cache_shapes,
        scratch_shapes=[pltpu.VMEM((ATT_GW // LANE, tm, LANE), F32)],
        compiler_params=_params(2),
        name="inproj",
    )(x, ada, ada, norm_g, w_all)


def _log_sigmoid(x):
    return jnp.minimum(x, 0.0) - jnp.log1p(jnp.exp(-jnp.abs(x)))


def _gla_kernel(qk_ref, v_ref, gr_ref, glr_ref, wg2_ref, bg_ref, gn_ref, o_ref, s_ref, st_ref):
    t = pl.program_id(1)
    c = GLA_CHUNK
    n_chunks = qk_ref.shape[1] // c

    @pl.when(t == 0)
    def _():
        st_ref[...] = jnp.zeros_like(st_ref)

    row = lax.broadcasted_iota(jnp.int32, (c, c), 0)
    col = lax.broadcasted_iota(jnp.int32, (c, c), 1)
    tril = row >= col
    tri_b = jnp.where(tril, 1.0, 0.0).astype(BF16)

    def chunk(ci, carry):
        r0 = pl.multiple_of(ci * c, c)
        rows = pl.ds(r0, c)
        logits = _dot(glr_ref[0, rows, :], wg2_ref[...]) + bg_ref[...]
        g = _log_sigmoid(logits) * (1.0 / GLA_GATE_NORM)
        b = _dot_exact_lhs_t(tri_b, g)
        bref = b[c // 2:c // 2 + 1, :]
        blast = b[c - 1:c, :]
        e_in = jnp.exp(b - bref)
        e_out = jnp.exp(bref - b)
        e_b = jnp.exp(b)
        e_k = jnp.exp(blast - b)
        e_last = jnp.exp(blast)
        qk = qk_ref[0, rows, :].astype(F32)
        q = qk[:, :GLA_QK_W] * (GLA_DK ** -0.5)
        k = qk[:, GLA_QK_W:]
        v = v_ref[0, rows, :]
        gr = gr_ref[0, rows, :].astype(F32)
        for h in range(GLA_HEADS):
            ks = slice(h * GLA_DK, (h + 1) * GLA_DK)
            vs = slice(h * GLA_DV, (h + 1) * GLA_DV)
            qh, kh, vh = q[:, ks], k[:, ks], v[:, vs]
            a = lax.dot_general((qh * e_in[:, ks]).astype(BF16), (kh * e_out[:, ks]).astype(BF16),
                                _NT, preferred_element_type=F32)
            a = jnp.where(tril, a, 0.0)
            st = st_ref[h]
            o = _dot(a.astype(BF16), vh) + lax.dot_general(
                (qh * e_b[:, ks]).astype(BF16), st.astype(BF16), _NT, preferred_element_type=F32)
            kd = (kh * e_k[:, ks]).astype(BF16)
            st_ref[h] = st * e_last[:, ks] + lax.dot_general(vh, kd, _TN, preferred_element_type=F32)
            grh = gr[:, vs]
            o_ref[0, rows, vs] = (_rms(o) * gn_ref[...] * _silu(grh)).astype(BF16)
        return carry

    lax.fori_loop(0, n_chunks, chunk, 0)

    @pl.when(t == pl.num_programs(1) - 1)
    def _():
        for h in range(GLA_HEADS):
            s_ref[0, h] = st_ref[h].T


def _dot_exact_lhs_t(m, x):
    a, b, c = _split3(x)
    return _dot(m, a) + _dot(m, b) + _dot(m, c)


def _gla_prompt(pm, wg2p, bg, gn, tc):
    bsz, t, _ = pm.shape
    blk = lambda k: pl.BlockSpec((1, tc, D_MODEL), functools.partial(lambda b, i, k: (b, i, k), k=k))
    return pl.pallas_call(
        _gla_kernel,
        grid=(bsz, t // tc),
        in_specs=[blk(0), blk(1), blk(2),
                  pl.BlockSpec((1, tc, GLR_PAD), lambda b, i: (b, i, (MAIN_W - GLR_PAD) // GLR_PAD)),
                  _resident(wg2p.shape), _resident(bg.shape), _resident(gn.shape)],
        out_specs=[pl.BlockSpec((1, tc, GLA_V_W), lambda b, i: (b, i, 0)),
                   pl.BlockSpec((1, GLA_HEADS, GLA_DK, GLA_DV), lambda b, i: (b, 0, 0, 0))],
        out_shape=[jax.ShapeDtypeStruct((bsz, t, GLA_V_W), BF16),
                   jax.ShapeDtypeStruct((bsz, GLA_HEADS, GLA_DK, GLA_DV), F32)],
        scratch_shapes=[pltpu.VMEM((GLA_HEADS, GLA_DV, GLA_DK), F32)],
        compiler_params=_params(2),
        name="gla_prompt",
    )(pm, pm, pm, pm, wg2p, bg, gn)


def _to_col(row):
    n = row.shape[1]
    eye = lax.broadcasted_iota(jnp.int32, (n, n), 0) == lax.broadcasted_iota(jnp.int32, (n, n), 1)
    return jnp.sum(jnp.where(eye, jnp.broadcast_to(row, (n, n)), 0.0), axis=1, keepdims=True)


def _gla_step_kernel(pm_ref, s0_ref, wg2_ref, bg_ref, gn_ref, o_ref, s_ref):
    p = pm_ref[0]
    glr = jnp.broadcast_to(p[:, MAIN_W - GLR_PAD:], (8, GLR_PAD))
    logits = _dot(glr, wg2_ref[...])[0:1] + bg_ref[...]
    eg = jnp.exp(_log_sigmoid(logits) * (1.0 / GLA_GATE_NORM))
    pf = p.astype(F32)
    q = pf[:, :GLA_QK_W] * (GLA_DK ** -0.5)
    k = pf[:, GLA_QK_W:2 * GLA_QK_W]
    v = pf[:, 2 * GLA_QK_W:2 * GLA_QK_W + GLA_V_W]
    gr = pf[:, 2 * GLA_QK_W + GLA_V_W:2 * GLA_QK_W + 2 * GLA_V_W]
    for h in range(GLA_HEADS):
        ks = slice(h * GLA_DK, (h + 1) * GLA_DK)
        vs = slice(h * GLA_DV, (h + 1) * GLA_DV)
        s_new = s0_ref[0, h] * _to_col(eg[:, ks]) + _to_col(k[:, ks]) * v[:, vs]
        s_ref[0, h] = s_new
        o = jnp.sum(_to_col(q[:, ks]) * s_new, axis=0, keepdims=True)
        o_ref[0, :, vs] = (_rms(o) * gn_ref[...] * _silu(gr[:, vs])).astype(BF16)


def _gla_sample(pm, s0, wg2p, bg, gn):
    n = pm.shape[0]
    st_spec = pl.BlockSpec((1, GLA_HEADS, GLA_DK, GLA_DV), lambda b: (b, 0, 0, 0))
    return pl.pallas_call(
        _gla_step_kernel,
        grid=(n,),
        in_specs=[pl.BlockSpec((1, 1, MAIN_W), lambda b: (b, 0, 0)), st_spec,
                  _resident(wg2p.shape), _resident(bg.shape), _resident(gn.shape)],
        out_specs=[pl.BlockSpec((1, 1, GLA_V_W), lambda b: (b, 0, 0)), st_spec],
        out_shape=[jax.ShapeDtypeStruct((n, 1, GLA_V_W), BF16),
                   jax.ShapeDtypeStruct(s0.shape, F32)],
        compiler_params=_params(1),
        name="gla_sample",
    )(pm, s0, wg2p, bg, gn)


def _att_kernel(q_ref, kp_ref, kc_ref, vp_ref, vc_ref, bias_ref, o_ref, lse_ref):
    first = jnp.where(pl.program_id(2) == 0, 1, 0)
    q = q_ref[...]
    kk = jnp.concatenate([kp_ref[...], kc_ref[...]], axis=0)
    vv = jnp.concatenate([vp_ref[...], vc_ref[...]], axis=0)
    lane = lax.broadcasted_iota(jnp.int32, (ATT_BLOCK, LANE), 1)
    low = lane < ATT_E
    half = (jnp.where(low, 1.0, 0.0).astype(BF16), jnp.where(low, 0.0, 1.0).astype(BF16))
    lse_tile = jnp.zeros((ATT_BLOCK, LANE), F32)
    for p in range(ATT_HPG // 2):
        cs = slice(p * LANE, (p + 1) * LANE)
        qp, kp, vp = q[:, cs], kk[:, cs], vv[:, cs]
        outs = []
        for hh in range(2):
            h = 2 * p + hh
            s = lax.dot_general(qp * half[hh], kp, _NT, preferred_element_type=F32)
            s = s * (ATT_E ** -0.5) + bias_ref[first, h]
            m = jnp.max(s, axis=-1, keepdims=True)
            e = jnp.exp(s - m)
            den = jnp.sum(e, axis=-1, keepdims=True)
            outs.append(_dot(e.astype(BF16), vp) / den)
            lse_tile = jnp.where(lane == h, m + jnp.log(den), lse_tile)
        o_ref[:, cs] = jnp.where(low, outs[0], outs[1]).astype(BF16)
    lse_ref[...] = lse_tile


def _att_prompt(pa, bias2, g):
    bsz, dil, l, _ = pa.shape
    nb = l // ATT_BLOCK

    def spec(part, prev):
        def imap(b, r, n):
            return (b, r, jnp.maximum(n - 1, 0) if prev else n, part)
        return pl.BlockSpec((None, None, ATT_BLOCK, ATT_GW), imap)

    out_spec = lambda width: pl.BlockSpec((None, None, ATT_BLOCK, width), lambda b, r, n: (b, r, n, 0))
    return pl.pallas_call(
        _att_kernel,
        grid=(bsz, dil, nb),
        in_specs=[spec(0, False), spec(1, True), spec(1, False), spec(2, True), spec(2, False),
                  _resident(bias2.shape)],
        out_specs=[out_spec(ATT_GW), out_spec(LANE)],
        out_shape=[jax.ShapeDtypeStruct((bsz, dil, l, ATT_GW), BF16),
                   jax.ShapeDtypeStruct((bsz, dil, l, LANE), F32)],
        compiler_params=_params(3),
        name=f"att_prompt_g{g}",
    )(pa, pa, pa, pa, pa, bias2)


def _att_step_kernel(q_ref, new_ref, buf_ref, sb_ref, sb0_ref, o_ref, l_ref, out_ref, p_ref, pn_ref):
    b = pl.program_id(0)
    kv = pl.program_id(1)
    lb = buf_ref.shape[-1]
    n = new_ref.shape[-1]
    scale = ATT_E ** -0.5
    qb = q_ref[...].astype(BF16)
    new = new_ref[kv]
    head = lax.broadcasted_iota(jnp.int32, (ATT_HPG, 1), 0)
    mine = lax.broadcasted_iota(jnp.int32, (1, n), 1) == b

    @pl.when(kv == 0)
    def _():
        s = jnp.zeros((ATT_HPG, lb), F32)
        sn = jnp.zeros((ATT_HPG, n), F32)
        for h in range(ATT_HPG):
            s = jnp.where(head == h, _dot(qb, buf_ref[h].astype(BF16)), s)
            sn = jnp.where(head == h, _dot(qb, new[h].astype(BF16)), sn)
        s = s * scale + sb_ref[...]
        sn = jnp.where(mine, sn * scale + sb0_ref[...], NEG_BIG)
        m = jnp.maximum(jnp.max(s, axis=-1, keepdims=True), jnp.max(sn, axis=-1, keepdims=True))
        e = jnp.exp(s - m)
        en = jnp.exp(sn - m)
        den = jnp.sum(e, axis=-1, keepdims=True) + jnp.sum(en, axis=-1, keepdims=True)
        p_ref[...] = e / den
        pn_ref[...] = en / den
        l_ref[...] = jnp.broadcast_to(m + jnp.log(den), (ATT_HPG, LANE))

    @pl.when(kv == 1)
    def _():
        pb = p_ref[...].astype(BF16)
        pnb = pn_ref[...].astype(BF16)
        o = jnp.zeros((ATT_HPG, ATT_E), F32)
        for h in range(ATT_HPG):
            oh = (lax.dot_general(pb, buf_ref[h].astype(BF16), _NT, preferred_element_type=F32)
                  + lax.dot_general(pnb, new[h].astype(BF16), _NT, preferred_element_type=F32))
            o = jnp.where(head == h, oh, o)
        o_ref[...] = o

    last = lax.broadcasted_iota(jnp.int32, (1, lb), 1) == lb - 1
    for h in range(ATT_HPG):
        col = jnp.sum(jnp.where(mine, new[h], 0.0), axis=-1, keepdims=True)
        out_ref[h] = jnp.where(last, col, pltpu.roll(buf_ref[h], lb - 1, 1))


def _att_sample(q, new_t, cache_t, sb, sb0, g):
    n, _, _, _, lb = cache_t.shape
    plane = pl.BlockSpec((None, None, ATT_HPG, ATT_E, lb), lambda b, kv: (b, kv, 0, 0, 0))
    per_seq = lambda width: pl.BlockSpec((None, ATT_HPG, width), lambda b, kv: (b, 0, 0))
    return pl.pallas_call(
        _att_step_kernel,
        grid=(n, 2),
        in_specs=[per_seq(ATT_E), _resident(new_t.shape), plane, _resident(sb.shape), _resident(sb0.shape)],
        out_specs=[per_seq(ATT_E), per_seq(LANE), plane],
        out_shape=[jax.ShapeDtypeStruct((n, ATT_HPG, ATT_E), F32),
                   jax.ShapeDtypeStruct((n, ATT_HPG, LANE), F32),
                   jax.ShapeDtypeStruct(cache_t.shape, F32)],
        scratch_shapes=[pltpu.VMEM((ATT_HPG, lb), F32), pltpu.VMEM((ATT_HPG, n), F32)],
        compiler_params=_params(2),
        name=f"att_sample_g{g}",
    )(q, new_t, cache_t, sb, sb0)


def _mix_ffn_kernel(*refs, per_row_conv):
    (x_ref, oa_ref, o1_ref, o2_ref, o3_ref, l1_ref, l2_ref, l3_ref, ga_ref, gb_ref,
     g1_ref, sh2_ref, sc2_ref, g2_ref, n2_ref, nf_ref, et_ref,
     wa_ref, wb_ref, wo_ref, wup_ref, cw_ref, cb_ref, wdn_ref) = refs[:24]
    if per_row_conv:
        cs0_ref, cs1_ref, y_ref, cv_ref, so_ref, sl_ref = refs[24:]
    else:
        y_ref, cv_ref, so_ref, sl_ref, carry_ref = refs[24:]
    tm = x_ref.shape[1]

    def token_order(ref, stage_ref):
        dil, _, width = ref.shape
        if dil == 1:
            return ref[0].astype(F32)
        for r in range(dil):
            plane = ref[r].astype(F32)
            for c in range(width // LANE):
                stage_ref[c, pl.ds(r, tm // dil, stride=dil), :] = plane[:, c * LANE:(c + 1) * LANE]
        return jnp.concatenate([stage_ref[c] for c in range(width // LANE)], axis=1)

    ls = [token_order(l_ref, sl_ref) for l_ref in (l1_ref, l2_ref, l3_ref)]
    m = jnp.maximum(jnp.maximum(ls[0], ls[1]), ls[2])
    ws = [jnp.exp(l - m) for l in ls]
    den = ws[0] + ws[1] + ws[2]
    o_att = jnp.zeros((tm, ATT_GW), F32)
    for w, o_ref in zip(ws, (o1_ref, o2_ref, o3_ref)):
        o_att = o_att + _dot_exact_lhs(w / den, et_ref[...]) * token_order(o_ref, so_ref)

    y_a = _dot(oa_ref[0], wa_ref[...])
    y_b = _dot(o_att.astype(BF16), wb_ref[...])
    mix_in = jax.nn.sigmoid(ga_ref[0].astype(F32)) * y_a + jax.nn.sigmoid(gb_ref[0].astype(F32)) * y_b
    x1 = x_ref[0] + g1_ref[0] * _dot(mix_in.astype(BF16), wo_ref[...])

    h2 = (_rms(x1) * n2_ref[...] * (1.0 + sc2_ref[0]) + sh2_ref[0]).astype(BF16)
    a = _dot(h2, wup_ref[:, :FFN_HIDDEN])
    bgate = _dot(h2, wup_ref[:, FFN_HIDDEN:])
    if per_row_conv:
        a2, a1 = cs0_ref[0], cs1_ref[0]
        cv_ref[0] = a
    else:
        @pl.when(pl.program_id(1) == 0)
        def _():
            carry_ref[...] = jnp.zeros_like(carry_ref)
        row = lax.broadcasted_iota(jnp.int32, (tm, FFN_HIDDEN), 0)
        prev = carry_ref[...]
        a1 = jnp.where(row == 0, prev[7:8], pltpu.roll(a, 1, 0))
        a2 = jnp.where(row == 0, prev[6:7], jnp.where(row == 1, prev[7:8], pltpu.roll(a, 2, 0)))
        carry_ref[...] = a[tm - 8:, :]
        cv_ref[0] = a[tm - 8:, :]
    a_conv = cb_ref[...] + (cw_ref[0:1] * a2 + cw_ref[1:2] * a1 + cw_ref[2:3] * a)
    y_f = _dot((_silu(a_conv) * bgate).astype(BF16), wdn_ref[...])
    x2 = x1 + g2_ref[0] * y_f
    y_ref[0] = _rms(x2) * nf_ref[...]


def _mix_ffn(x, oa, os_, ls_, pm, ada, n2g, nfg, emat_t, wa, wb, wo, wup, cw, cb, wdn, tm, conv_state=None):
    bsz, t, _ = x.shape
    per_row = conv_state is not None
    rows = lambda width, k=0: pl.BlockSpec(
        (1, tm, width), functools.partial(lambda b, i, k: (b, i, k), k=k))
    planes = lambda a: pl.BlockSpec((None, a.shape[1], tm // a.shape[1], a.shape[3]),
                                    lambda b, i: (b, 0, i, 0))
    in_specs = ([rows(D_MODEL), rows(GLA_V_W)] + [planes(a) for a in (*os_, *ls_)]
                + [rows(D_MODEL, 3), rows(D_MODEL, 4)]
                + [_ada_spec(ada, tm, k) for k in (2, 3, 4, 5)]
                + [_resident(a.shape) for a in (n2g, nfg, emat_t, wa, wb, wo, wup, cw, cb, wdn)])
    args = [x, oa, *os_, *ls_, pm, pm, ada, ada, ada, ada, n2g, nfg, emat_t, wa, wb, wo, wup, cw, cb, wdn]
    scratch = [pltpu.VMEM((ATT_GW // LANE, tm, LANE), F32), pltpu.VMEM((1, tm, LANE), F32)]
    if per_row:
        in_specs += [rows(FFN_HIDDEN)] * 2
        args += [conv_state[:, 0][None], conv_state[:, 1][None]]
        cv_spec = rows(FFN_HIDDEN)
        cv_shape = jax.ShapeDtypeStruct((bsz, t, FFN_HIDDEN), F32)
    else:
        cv_spec = pl.BlockSpec((1, 8, FFN_HIDDEN), lambda b, i: (b, 0, 0))
        cv_shape = jax.ShapeDtypeStruct((bsz, 8, FFN_HIDDEN), F32)
        scratch.append(pltpu.VMEM((8, FFN_HIDDEN), F32))
    return pl.pallas_call(
        functools.partial(_mix_ffn_kernel, per_row_conv=per_row),
        grid=(bsz, t // tm),
        in_specs=in_specs,
        out_specs=[rows(D_MODEL), cv_spec],
        out_shape=[jax.ShapeDtypeStruct((bsz, t, D_MODEL), F32), cv_shape],
        scratch_shapes=scratch,
        compiler_params=_params(2),
        name="mix_ffn_sample" if per_row else "mix_ffn_prompt",
    )(*args)


def _t5_bucket(dist):
    max_exact = REL_BUCKETS // 2
    d = np.maximum(dist, 1).astype(np.float32)
    large = max_exact + (np.log(d / max_exact) / np.log(REL_MAX_DIST / max_exact)
                         * (REL_BUCKETS - max_exact)).astype(np.int32)
    large = np.minimum(large, REL_BUCKETS - 1)
    return np.where(dist < max_exact, dist, large).astype(np.int32)


def _prompt_bias(rel_bias_g, dil):
    i = np.arange(ATT_BLOCK)[:, None]
    j = np.arange(2 * ATT_BLOCK)[None, :]
    off = ATT_BLOCK + i - j
    valid = (off >= 0) & (off <= ATT_SPAN)
    bucket = _t5_bucket(np.clip(off, 0, ATT_SPAN) * dil)
    bias = jnp.transpose(rel_bias_g[bucket], (2, 0, 1))
    normal = jnp.where(valid[None], bias, NEG_BIG)
    first = jnp.where((valid & (j >= ATT_BLOCK))[None], bias, NEG_BIG)
    return jnp.stack([normal, first])


def _sample_bias(rel_bias_g, lb, dil):
    dist = lb - np.arange(lb)
    valid = (dist % dil == 0) & (dist <= ATT_SPAN * dil)
    sb = jnp.where(valid[None], rel_bias_g[_t5_bucket(dist)].T, NEG_BIG)
    sb0 = rel_bias_g[_t5_bucket(np.zeros((1,), np.int64))].T
    return sb, sb0


def _head_indicator_t():
    e = (np.arange(LANE)[:, None] == np.arange(ATT_GW)[None, :] // ATT_E).astype(np.float32)
    return jnp.asarray(e, BF16)


def _permute_w_in(w_in):
    sizes = (GLA_QK_W, GLA_QK_W, GLA_V_W, GLA_V_W, GLA_GATE_RANK, 3 * ATT_GW, 3 * ATT_GW, 3 * ATT_GW,
             D_MODEL, D_MODEL)
    offs = np.concatenate([[0], np.cumsum(sizes)])
    seg = lambda i: w_in[:, offs[i]:offs[i + 1]]
    gq, gk, gv, gr, glr, aq, ak, av, ga, gb = [seg(i) for i in range(10)]
    glr = jnp.pad(glr, ((0, 0), (0, GLR_PAD - GLA_GATE_RANK)))
    att = []
    for g in range(len(ATT_GROUPS)):
        cs = slice(g * ATT_GW, (g + 1) * ATT_GW)
        att += [aq[:, cs], ak[:, cs], av[:, cs]]
    return jnp.concatenate([gq, gk, gv, gr, ga, gb, glr] + att, axis=1).astype(BF16)


PROMPT_TM = 256
SAMPLE_TM = 128
GLA_TC = 512


def kernel(x_prompt, x_sample, state_gla, cache_win1, cache_win2, cache_win3, state_conv, c_prompt, c_sample,
           w_ada, b_ada, norm1_g, w_in, gla_wg2, gla_bg, gla_norm_g, w_branch_a, w_branch_b, w_out, rel_bias,
           norm2_g, w_up, conv_w, conv_b, w_down, normf_g):
    bp, tp, _ = x_prompt.shape
    ns = x_sample.shape[0]
    assert w_ada.shape[0] == 1, "single-layer trunk"

    w_all = _permute_w_in(w_in[0])
    wg2p = jnp.pad(gla_wg2[0], ((0, GLR_PAD - GLA_GATE_RANK), (0, 0))).astype(BF16)
    bg = gla_bg[0][None]
    gn = gla_norm_g[0][None]
    n1g, n2g, nfg = norm1_g[0][None], norm2_g[0][None], normf_g[None]
    wa, wb, wo = w_branch_a[0].astype(BF16), w_branch_b[0].astype(BF16), w_out[0].astype(BF16)
    wup, wdn = w_up[0].astype(BF16), w_down[0].astype(BF16)
    cw, cb = conv_w[0], conv_b[0][None]
    emat_t = _head_indicator_t()
    groups = range(len(ATT_GROUPS))
    rel_g = [rel_bias[:, g * ATT_HPG:(g + 1) * ATT_HPG] for g in groups]

    ada = _ada(jnp.concatenate([c_prompt, c_sample], axis=0), w_ada[0].astype(BF16), b_ada[0][None])
    ada_p = ada[:bp][:, None, :]
    ada_s = ada[bp:][None]
    xs = x_sample.reshape(1, ns, D_MODEL)

    dils = tuple(d for _, d in ATT_GROUPS)
    pm_p, *rest = _inproj(x_prompt, ada_p, n1g, w_all, PROMPT_TM, tuple(min(w, tp) for w, _ in ATT_GROUPS), dils)
    pa_p, tails_p = rest[:3], rest[3:]
    oa_p, gla_p = _gla_prompt(pm_p, wg2p, bg, gn, GLA_TC)
    os_p, ls_p = [], []
    for g in groups:
        o, l = _att_prompt(pa_p[g], _prompt_bias(rel_g[g], dils[g]), g)
        os_p.append(o)
        ls_p.append(l)
    y_p, cv_p = _mix_ffn(x_prompt, oa_p, os_p, ls_p, pm_p, ada_p, n2g, nfg, emat_t,
                         wa, wb, wo, wup, cw, cb, wdn, PROMPT_TM)

    pm_s, *rest = _inproj(xs, ada_s, n1g, w_all, SAMPLE_TM, (ns, ns, ns), (1, 1, 1))
    pa_s, new_t = rest[:3], rest[3:]
    oa_s, gla_s = _gla_sample(pm_s.reshape(ns, 1, MAIN_W), state_gla[0], wg2p, bg, gn)
    os_s, ls_s, new_caches = [], [], []
    for g, cache in enumerate((cache_win1, cache_win2, cache_win3)):
        lb = cache.shape[2]
        cache_t = jnp.transpose(cache[0], (0, 2, 3, 4, 1))
        q = pa_s[g][0, 0, :, :ATT_GW].astype(F32).reshape(ns, ATT_HPG, ATT_E)
        sb, sb0 = _sample_bias(rel_g[g], lb, dils[g])
        o, l, shifted = _att_sample(q, new_t[g][0], cache_t, sb, sb0, g)
        os_s.append(o.reshape(1, 1, ns, ATT_GW).astype(BF16))
        ls_s.append(jnp.pad(l[:, :, 0], ((0, 0), (0, LANE - ATT_HPG)))[None, None])
        new_caches.append(jnp.transpose(shifted, (0, 4, 1, 2, 3))[None])
    y_s, cv_s = _mix_ffn(xs, oa_s.reshape(1, ns, GLA_V_W), os_s, ls_s, pm_s, ada_s, n2g, nfg, emat_t,
                         wa, wb, wo, wup, cw, cb, wdn, SAMPLE_TM, conv_state=state_conv[0])

    tails = [jnp.transpose(c, (0, 4, 1, 2, 3))[None] for c in tails_p]
    conv_s = jnp.stack([state_conv[0][:, 1], cv_s[0]], axis=1)[None]
    return (y_p, y_s.reshape(ns, 1, D_MODEL), gla_p[None], gla_s[None],
            tails[0], new_caches[0], tails[1], new_caches[1], tails[2], new_caches[2],
            cv_p[:, 6:8][None], conv_s)
```

```python
import functools

import numpy as np
import jax
import jax.numpy as jnp
from jax import lax
from jax.experimental import pallas as pl
from jax.experimental.pallas import tpu as pltpu

F32 = jnp.float32
BF16 = jnp.bfloat16

D_MODEL = 1024
GLA_HEADS = 4
GLA_DK = 128
GLA_DV = 256
GLA_QK_W = GLA_HEADS * GLA_DK
GLA_V_W = GLA_HEADS * GLA_DV
GLA_GATE_RANK = 16
GLA_GATE_NORM = 16.0
GLA_CHUNK = 64
ATT_GROUPS = ((128, 1), (512, 4), (2048, 16))
ATT_HPG = 8
ATT_E = 64
ATT_GW = ATT_HPG * ATT_E
ATT_BLOCK = 128
ATT_SPAN = 128
REL_BUCKETS = 32
REL_MAX_DIST = 2048
FFN_HIDDEN = 2816
N_ADA = 6
EPS = 1e-6
NEG_BIG = -1e30

LANE = 128
GLR_PAD = LANE
MAIN_W = 2 * GLA_QK_W + 2 * GLA_V_W + 2 * D_MODEL + GLR_PAD
ATT_W3 = 3 * 3 * ATT_GW
COL_CHUNK = 512
VMEM_LIMIT = 56 * 1024 * 1024

_NT = (((1,), (1,)), ((), ()))
_TN = (((0,), (0,)), ((), ()))


def _params(n_axes):
    return pltpu.CompilerParams(dimension_semantics=("arbitrary",) * n_axes,
                                vmem_limit_bytes=VMEM_LIMIT)


def _resident(shape):
    zeros = (0,) * len(shape)
    return pl.BlockSpec(shape, lambda *_: zeros, pipeline_mode=pl.Buffered(1))


def _silu(x):
    return x * jax.nn.sigmoid(x)


def _rms(x):
    return x * lax.rsqrt(jnp.mean(x * x, axis=-1, keepdims=True) + EPS)


def _split3(x):
    a = x.astype(BF16)
    r = x - a.astype(F32)
    b = r.astype(BF16)
    c = (r - b.astype(F32)).astype(BF16)
    return a, b, c


def _dot(a, b):
    return jnp.dot(a, b, preferred_element_type=F32)


def _dot_exact_lhs(x, m, terms=3):
    parts = _split3(x)[:terms]
    out = _dot(parts[0], m)
    for p in parts[1:]:
        out = out + _dot(p, m)
    return out


def _ada_kernel(c_ref, w_ref, b_ref, o_ref):
    s = _silu(c_ref[...]).astype(BF16)
    o_ref[...] = _dot(s, w_ref[...]) + b_ref[...]


def _ada(c_all, w_ada, b_ada):
    m = c_all.shape[0]
    n = w_ada.shape[1]
    tn = D_MODEL
    return pl.pallas_call(
        _ada_kernel,
        grid=(n // tn,),
        in_specs=[pl.BlockSpec((m, D_MODEL), lambda j: (0, 0)),
                  pl.BlockSpec((D_MODEL, tn), lambda j: (0, j)),
                  pl.BlockSpec((1, tn), lambda j: (0, j))],
        out_specs=pl.BlockSpec((m, tn), lambda j: (0, j)),
        out_shape=jax.ShapeDtypeStruct((m, n), F32),
        compiler_params=_params(1),
        name="ada",
    )(c_all, w_ada, b_ada)


def _ada_spec(ada, tm, k):
    if ada.shape[1] == 1:
        return pl.BlockSpec((1, 1, D_MODEL), lambda b, i: (b, 0, k))
    return pl.BlockSpec((1, tm, D_MODEL), lambda b, i: (b, i, k))


def _inproj_kernel(x_ref, sh_ref, sc_ref, g_ref, w_ref, pm_ref, pa1_ref, pa2_ref, pa3_ref,
                   c1_ref, c2_ref, c3_ref, stage_ref):
    tm = x_ref.shape[1]
    h = _rms(x_ref[0]) * g_ref[...]
    h = h * (1.0 + sc_ref[0]) + sh_ref[0]
    hb = h.astype(BF16)
    for c0 in range(0, MAIN_W, COL_CHUNK):
        c1 = min(c0 + COL_CHUNK, MAIN_W)
        pm_ref[0, :, c0:c1] = _dot(hb, w_ref[:, c0:c1]).astype(BF16)
    for g, (pa_ref, cache_ref) in enumerate(zip((pa1_ref, pa2_ref, pa3_ref), (c1_ref, c2_ref, c3_ref))):
        dil = pa_ref.shape[0]
        rows = cache_ref.shape[-1]
        for part in range(3):
            cs = slice(part * ATT_GW, (part + 1) * ATT_GW)
            c0 = MAIN_W + (3 * g + part) * ATT_GW
            acc = _dot(hb, w_ref[:, c0:c0 + ATT_GW])
            if dil == 1:
                pa_ref[0, :, cs] = acc.astype(BF16)
            else:
                stage = stage_ref.at[3 * g + part]
                for c in range(ATT_GW // LANE):
                    stage[c] = acc[:, c * LANE:(c + 1) * LANE]
                for r in range(dil):
                    for c in range(ATT_GW // LANE):
                        c0 = part * ATT_GW + c * LANE
                        pa_ref[r, :, c0:c0 + LANE] = stage[c, pl.ds(r, tm // dil, stride=dil), :].astype(BF16)
            if part:
                tail = acc[tm - rows:, :].T
                cache_ref[part - 1] = tail.reshape(ATT_HPG, ATT_E, rows)


def _inproj(x, ada, norm_g, w_all, tm, windows, dils):
    bsz, t, _ = x.shape
    nt = t // tm
    pa_specs, pa_shapes, cache_specs, cache_shapes = [], [], [], []
    for w, dil in zip(windows, dils):
        rows = min(tm, w)
        nblk = w // rows
        pa_specs.append(pl.BlockSpec((None, dil, tm // dil, 3 * ATT_GW), lambda b, i: (b, 0, i, 0)))
        pa_shapes.append(jax.ShapeDtypeStruct((bsz, dil, t // dil, 3 * ATT_GW), BF16))
        cache_specs.append(pl.BlockSpec(
            (None, 2, ATT_HPG, ATT_E, rows),
            functools.partial(lambda b, i, first: (b, 0, 0, 0, jnp.maximum(i - first, 0)), first=nt - nblk)))
        cache_shapes.append(jax.ShapeDtypeStruct((bsz, 2, ATT_HPG, ATT_E, w), F32))
    return pl.pallas_call(
        _inproj_kernel,
        grid=(bsz, nt),
        in_specs=[pl.BlockSpec((1, tm, D_MODEL), lambda b, i: (b, i, 0)),
                  _ada_spec(ada, tm, 0), _ada_spec(ada, tm, 1),
                  _resident((1, D_MODEL)), _resident(w_all.shape)],
        out_specs=[pl.BlockSpec((1, tm, MAIN_W), lambda b, i: (b, i, 0))] + pa_specs + cache_specs,
        out_shape=[jax.ShapeDtypeStruct((bsz, t, MAIN_W), BF16)] + pa_shapes + cache_shapes,
        scratch_shapes=[pltpu.VMEM((3 * len(windows), ATT_GW // LANE, tm, LANE), F32)],
        compiler_params=_params(2),
        name="inproj",
    )(x, ada, ada, norm_g, w_all)


def _log_sigmoid(x):
    return jnp.minimum(x, 0.0) - jnp.log1p(jnp.exp(-jnp.abs(x)))


def _gla_kernel(qk_ref, v_ref, gr_ref, glr_ref, wg2_ref, bg_ref, gn_ref, o_ref, s_ref, st_ref):
    t = pl.program_id(1)
    c = GLA_CHUNK
    n_chunks = qk_ref.shape[1] // c

    @pl.when(t == 0)
    def _():
        st_ref[...] = jnp.zeros_like(st_ref)

    row = lax.broadcasted_iota(jnp.int32, (c, c), 0)
    col = lax.broadcasted_iota(jnp.int32, (c, c), 1)
    tril = row >= col
    tri_b = jnp.where(tril, 1.0, 0.0).astype(BF16)

    def chunk(ci, carry):
        r0 = pl.multiple_of(ci * c, c)
        rows = pl.ds(r0, c)
        logits = _dot(glr_ref[0, rows, :], wg2_ref[...]) + bg_ref[...]
        g = _log_sigmoid(logits) * (1.0 / GLA_GATE_NORM)
        b = _dot_exact_lhs_t(tri_b, g)
        bref = b[c // 2:c // 2 + 1, :]
        blast = b[c - 1:c, :]
        e_in = jnp.exp(b - bref)
        e_out = jnp.exp(bref - b)
        e_b = jnp.exp(b)
        e_k = jnp.exp(blast - b)
        e_last = jnp.exp(blast)
        qk = qk_ref[0, rows, :].astype(F32)
        q = qk[:, :GLA_QK_W] * (GLA_DK ** -0.5)
        k = qk[:, GLA_QK_W:]
        v = v_ref[0, rows, :]
        gr = gr_ref[0, rows, :].astype(F32)
        for h in range(GLA_HEADS):
            ks = slice(h * GLA_DK, (h + 1) * GLA_DK)
            vs = slice(h * GLA_DV, (h + 1) * GLA_DV)
            qh, kh, vh = q[:, ks], k[:, ks], v[:, vs]
            a = lax.dot_general((qh * e_in[:, ks]).astype(BF16), (kh * e_out[:, ks]).astype(BF16),
                                _NT, preferred_element_type=F32)
            a = jnp.where(tril, a, 0.0)
            st = st_ref[h]
            o = _dot(a.astype(BF16), vh) + lax.dot_general(
                (qh * e_b[:, ks]).astype(BF16), st.astype(BF16), _NT, preferred_element_type=F32)
            kd = (kh * e_k[:, ks]).astype(BF16)
            st_ref[h] = st * e_last[:, ks] + lax.dot_general(vh, kd, _TN, preferred_element_type=F32)
            grh = gr[:, vs]
            o_ref[0, rows, vs] = (_rms(o) * gn_ref[...] * _silu(grh)).astype(BF16)
        return carry

    lax.fori_loop(0, n_chunks, chunk, 0, unroll=True)

    @pl.when(t == pl.num_programs(1) - 1)
    def _():
        for h in range(GLA_HEADS):
            s_ref[0, h] = st_ref[h].T


def _dot_exact_lhs_t(m, x):
    a, b, c = _split3(x)
    return _dot(m, a) + _dot(m, b) + _dot(m, c)


def _gla_prompt(pm, wg2p, bg, gn, tc):
    bsz, t, _ = pm.shape
    blk = lambda k: pl.BlockSpec((1, tc, D_MODEL), functools.partial(lambda b, i, k: (b, i, k), k=k))
    return pl.pallas_call(
        _gla_kernel,
        grid=(bsz, t // tc),
        in_specs=[blk(0), blk(1), blk(2),
                  pl.BlockSpec((1, tc, GLR_PAD), lambda b, i: (b, i, (MAIN_W - GLR_PAD) // GLR_PAD)),
                  _resident(wg2p.shape), _resident(bg.shape), _resident(gn.shape)],
        out_specs=[pl.BlockSpec((1, tc, GLA_V_W), lambda b, i: (b, i, 0)),
                   pl.BlockSpec((1, GLA_HEADS, GLA_DK, GLA_DV), lambda b, i: (b, 0, 0, 0))],
        out_shape=[jax.ShapeDtypeStruct((bsz, t, GLA_V_W), BF16),
                   jax.ShapeDtypeStruct((bsz, GLA_HEADS, GLA_DK, GLA_DV), F32)],
        scratch_shapes=[pltpu.VMEM((GLA_HEADS, GLA_DV, GLA_DK), F32)],
        compiler_params=_params(2),
        name="gla_prompt",
    )(pm, pm, pm, pm, wg2p, bg, gn)


def _to_col(row):
    n = row.shape[1]
    eye = lax.broadcasted_iota(jnp.int32, (n, n), 0) == lax.broadcasted_iota(jnp.int32, (n, n), 1)
    return jnp.sum(jnp.where(eye, jnp.broadcast_to(row, (n, n)), 0.0), axis=1, keepdims=True)


def _gla_step_kernel(pm_ref, s0_ref, wg2_ref, bg_ref, gn_ref, o_ref, s_ref):
    p = pm_ref[0]
    glr = jnp.broadcast_to(p[:, MAIN_W - GLR_PAD:], (8, GLR_PAD))
    logits = _dot(glr, wg2_ref[...])[0:1] + bg_ref[...]
    eg = jnp.exp(_log_sigmoid(logits) * (1.0 / GLA_GATE_NORM))
    pf = p.astype(F32)
    q = pf[:, :GLA_QK_W] * (GLA_DK ** -0.5)
    k = pf[:, GLA_QK_W:2 * GLA_QK_W]
    v = pf[:, 2 * GLA_QK_W:2 * GLA_QK_W + GLA_V_W]
    gr = pf[:, 2 * GLA_QK_W + GLA_V_W:2 * GLA_QK_W + 2 * GLA_V_W]
    for h in range(GLA_HEADS):
        ks = slice(h * GLA_DK, (h + 1) * GLA_DK)
        vs = slice(h * GLA_DV, (h + 1) * GLA_DV)
        s_new = s0_ref[0, h] * _to_col(eg[:, ks]) + _to_col(k[:, ks]) * v[:, vs]
        s_ref[0, h] = s_new
        o = jnp.sum(_to_col(q[:, ks]) * s_new, axis=0, keepdims=True)
        o_ref[0, :, vs] = (_rms(o) * gn_ref[...] * _silu(gr[:, vs])).astype(BF16)


def _gla_sample(pm, s0, wg2p, bg, gn):
    n = pm.shape[0]
    st_spec = pl.BlockSpec((1, GLA_HEADS, GLA_DK, GLA_DV), lambda b: (b, 0, 0, 0))
    return pl.pallas_call(
        _gla_step_kernel,
        grid=(n,),
        in_specs=[pl.BlockSpec((1, 1, MAIN_W), lambda b: (b, 0, 0)), st_spec,
                  _resident(wg2p.shape), _resident(bg.shape), _resident(gn.shape)],
        out_specs=[pl.BlockSpec((1, 1, GLA_V_W), lambda b: (b, 0, 0)), st_spec],
        out_shape=[jax.ShapeDtypeStruct((n, 1, GLA_V_W), BF16),
                   jax.ShapeDtypeStruct(s0.shape, F32)],
        compiler_params=_params(1),
        name="gla_sample",
    )(pm, s0, wg2p, bg, gn)


ATT_MAX_STEP_BLOCKS = 4


def _att_kernel(q_ref, kp_ref, kc_ref, vp_ref, vc_ref, bias_ref, o_ref, lse_ref):
    kk = jnp.concatenate([kp_ref[...], kc_ref[...]], axis=0)
    vv = jnp.concatenate([vp_ref[...], vc_ref[...]], axis=0)
    lane = lax.broadcasted_iota(jnp.int32, (ATT_BLOCK, LANE), 1)
    low = lane < ATT_E
    scale = ATT_E ** -0.5
    half = (jnp.where(low, scale, 0.0).astype(BF16), jnp.where(low, 0.0, scale).astype(BF16))
    for blk in range(q_ref.shape[0] // ATT_BLOCK):
        first = jnp.where(pl.program_id(2) == 0, 1, 0) if blk == 0 else 0
        rows = slice(blk * ATT_BLOCK, (blk + 1) * ATT_BLOCK)
        keys = slice(blk * ATT_BLOCK, (blk + 2) * ATT_BLOCK)
        lse_tile = jnp.zeros((ATT_BLOCK, LANE), F32)
        for p in range(ATT_HPG // 2):
            cs = slice(p * LANE, (p + 1) * LANE)
            qp, kp, vp = q_ref[rows, cs], kk[keys, cs], vv[keys, cs]
            outs = []
            for hh in range(2):
                h = 2 * p + hh
                s = lax.dot_general(qp * half[hh], kp, _NT, preferred_element_type=F32)
                s = s + bias_ref[first, h]
                m = jnp.max(s, axis=-1, keepdims=True)
                e = jnp.exp(s - m)
                den = jnp.sum(e, axis=-1, keepdims=True)
                outs.append(_dot(e.astype(BF16), vp) / den)
                lse_tile = jnp.where(lane == h, m + jnp.log(den), lse_tile)
            o_ref[rows, cs] = jnp.where(low, outs[0], outs[1]).astype(BF16)
        lse_ref[rows, :] = lse_tile


def _att_prompt(pa, bias2, g):
    bsz, dil, l, _ = pa.shape
    step_blocks = min(ATT_MAX_STEP_BLOCKS, l // ATT_BLOCK)
    step_rows = step_blocks * ATT_BLOCK
    own = lambda part: pl.BlockSpec((None, None, step_rows, ATT_GW),
                                    functools.partial(lambda b, r, n, part: (b, r, n, part), part=part))
    prev = lambda part: pl.BlockSpec(
        (None, None, ATT_BLOCK, ATT_GW),
        functools.partial(lambda b, r, n, part: (b, r, jnp.maximum(step_blocks * n - 1, 0), part), part=part))
    out_spec = lambda width: pl.BlockSpec((None, None, step_rows, width), lambda b, r, n: (b, r, n, 0))
    return pl.pallas_call(
        _att_kernel,
        grid=(bsz, dil, l // step_rows),
        in_specs=[own(0), prev(1), own(1), prev(2), own(2), _resident(bias2.shape)],
        out_specs=[out_spec(ATT_GW), out_spec(LANE)],
        out_shape=[jax.ShapeDtypeStruct((bsz, dil, l, ATT_GW), BF16),
                   jax.ShapeDtypeStruct((bsz, dil, l, LANE), F32)],
        compiler_params=_params(3),
        name=f"att_prompt_g{g}",
    )(pa, pa, pa, pa, pa, bias2)


def _att_step_kernel(q_ref, new_ref, buf_ref, sb_ref, sb0_ref, o_ref, l_ref, out_ref):
    seqs, _, _, _, lb = buf_ref.shape
    n = new_ref.shape[-1]
    scale = ATT_E ** -0.5
    head = lax.broadcasted_iota(jnp.int32, (ATT_HPG, 1), 0)
    last = lax.broadcasted_iota(jnp.int32, (1, lb), 1) == lb - 1
    for j in range(seqs):
        b = pl.program_id(0) * seqs + j
        mine = lax.broadcasted_iota(jnp.int32, (1, n), 1) == b
        qb = q_ref[j].astype(BF16)
        s = jnp.zeros((ATT_HPG, lb), F32)
        sn = jnp.zeros((ATT_HPG, n), F32)
        for h in range(ATT_HPG):
            s = jnp.where(head == h, _dot(qb, buf_ref[j, 0, h].astype(BF16)), s)
            sn = jnp.where(head == h, _dot(qb, new_ref[0, h].astype(BF16)), sn)
        s = s * scale + sb_ref[...]
        sn = jnp.where(mine, sn * scale + sb0_ref[...], NEG_BIG)
        m = jnp.maximum(jnp.max(s, axis=-1, keepdims=True), jnp.max(sn, axis=-1, keepdims=True))
        e = jnp.exp(s - m)
        en = jnp.exp(sn - m)
        den = jnp.sum(e, axis=-1, keepdims=True) + jnp.sum(en, axis=-1, keepdims=True)
        pb = (e / den).astype(BF16)
        pnb = (en / den).astype(BF16)
        o = jnp.zeros((ATT_HPG, ATT_E), F32)
        for h in range(ATT_HPG):
            oh = (lax.dot_general(pb, buf_ref[j, 1, h].astype(BF16), _NT, preferred_element_type=F32)
                  + lax.dot_general(pnb, new_ref[1, h].astype(BF16), _NT, preferred_element_type=F32))
            o = jnp.where(head == h, oh, o)
        o_ref[j] = o
        l_ref[j] = jnp.broadcast_to(m + jnp.log(den), (ATT_HPG, LANE))
        for kv in range(2):
            for h in range(ATT_HPG):
                col = jnp.sum(jnp.where(mine, new_ref[kv, h], 0.0), axis=-1, keepdims=True)
                out_ref[j, kv, h] = jnp.where(last, col, pltpu.roll(buf_ref[j, kv, h], lb - 1, 1))


ATT_SAMPLE_STEP_BYTES = 8 * 1024 * 1024


def _att_sample(q, new_t, cache_t, sb, sb0, g):
    n, _, _, _, lb = cache_t.shape
    seqs = max(1, min(8, ATT_SAMPLE_STEP_BYTES // (2 * ATT_GW * lb * 4)))
    planes = pl.BlockSpec((seqs, 2, ATT_HPG, ATT_E, lb), lambda i: (i, 0, 0, 0, 0))
    per_seq = lambda width: pl.BlockSpec((seqs, ATT_HPG, width), lambda i: (i, 0, 0))
    return pl.pallas_call(
        _att_step_kernel,
        grid=(n // seqs,),
        in_specs=[per_seq(ATT_E), _resident(new_t.shape), planes, _resident(sb.shape), _resident(sb0.shape)],
        out_specs=[per_seq(ATT_E), per_seq(LANE), planes],
        out_shape=[jax.ShapeDtypeStruct((n, ATT_HPG, ATT_E), F32),
                   jax.ShapeDtypeStruct((n, ATT_HPG, LANE), F32),
                   jax.ShapeDtypeStruct(cache_t.shape, F32)],
        compiler_params=_params(1),
        name=f"att_sample_g{g}",
    )(q, new_t, cache_t, sb, sb0)


def _mix_ffn_kernel(*refs, per_row_conv):
    (x_ref, oa_ref, o1_ref, o2_ref, o3_ref, l1_ref, l2_ref, l3_ref, ga_ref, gb_ref,
     g1_ref, sh2_ref, sc2_ref, g2_ref, n2_ref, nf_ref, et_ref,
     wa_ref, wb_ref, wo_ref, wup_ref, cw_ref, cb_ref, wdn_ref) = refs[:24]
    if per_row_conv:
        cs0_ref, cs1_ref, y_ref, cv_ref, so_ref, sl_ref = refs[24:]
    else:
        y_ref, cv_ref, so_ref, sl_ref, carry_ref = refs[24:]

        @pl.when(pl.program_id(1) == 0)
        def _():
            carry_ref[...] = jnp.zeros_like(carry_ref)
    tm = x_ref.shape[1]

    def token_order(ref, stage_ref):
        dil, _, width = ref.shape
        if dil == 1:
            return ref[0].astype(F32)
        for r in range(dil):
            plane = ref[r].astype(F32)
            for c in range(width // LANE):
                stage_ref[c, pl.ds(r, tm // dil, stride=dil), :] = plane[:, c * LANE:(c + 1) * LANE]
        return jnp.concatenate([stage_ref[c] for c in range(width // LANE)], axis=1)

    ls = [token_order(l_ref, sl_ref) for l_ref in (l1_ref, l2_ref, l3_ref)]
    m = jnp.maximum(jnp.maximum(ls[0], ls[1]), ls[2])
    ws = [jnp.exp(l - m) for l in ls]
    den = ws[0] + ws[1] + ws[2]
    o_att = jnp.zeros((tm, ATT_GW), F32)
    for w, o_ref in zip(ws, (o1_ref, o2_ref, o3_ref)):
        o_att = o_att + _dot_exact_lhs(w / den, et_ref[...], terms=2) * token_order(o_ref, so_ref)

    y_a = _dot(oa_ref[0], wa_ref[...])
    y_b = _dot(o_att.astype(BF16), wb_ref[...])
    mix_in = jax.nn.sigmoid(ga_ref[0].astype(F32)) * y_a + jax.nn.sigmoid(gb_ref[0].astype(F32)) * y_b
    x1 = x_ref[0] + g1_ref[0] * _dot(mix_in.astype(BF16), wo_ref[...])

    h2 = (_rms(x1) * n2_ref[...] * (1.0 + sc2_ref[0]) + sh2_ref[0]).astype(BF16)
    a = _dot(h2, wup_ref[:, :FFN_HIDDEN])
    bgate = _dot(h2, wup_ref[:, FFN_HIDDEN:])
    if per_row_conv:
        a2, a1 = cs0_ref[0], cs1_ref[0]
        cv_ref[0] = a
    else:
        row = lax.broadcasted_iota(jnp.int32, (tm, FFN_HIDDEN), 0)
        prev = carry_ref[...]
        a1 = jnp.where(row == 0, prev[7:8], pltpu.roll(a, 1, 0))
        a2 = jnp.where(row == 0, prev[6:7], jnp.where(row == 1, prev[7:8], pltpu.roll(a, 2, 0)))
        carry_ref[...] = a[tm - 8:, :]
        cv_ref[0] = a[tm - 8:, :]
    a_conv = cb_ref[...] + (cw_ref[0:1] * a2 + cw_ref[1:2] * a1 + cw_ref[2:3] * a)
    y_f = _dot((_silu(a_conv) * bgate).astype(BF16), wdn_ref[...])
    x2 = x1 + g2_ref[0] * y_f
    y_ref[0] = _rms(x2) * nf_ref[...]


def _mix_ffn(x, oa, os_, ls_, pm, ada, n2g, nfg, emat_t, wa, wb, wo, wup, cw, cb, wdn, tm, conv_state=None):
    bsz, t, _ = x.shape
    per_row = conv_state is not None
    rows = lambda width, k=0: pl.BlockSpec(
        (1, tm, width), functools.partial(lambda b, i, k: (b, i, k), k=k))
    planes = lambda a: pl.BlockSpec((None, a.shape[1], tm // a.shape[1], a.shape[3]),
                                    lambda b, i: (b, 0, i, 0))
    in_specs = ([rows(D_MODEL), rows(GLA_V_W)] + [planes(a) for a in (*os_, *ls_)]
                + [rows(D_MODEL, 3), rows(D_MODEL, 4)]
                + [_ada_spec(ada, tm, k) for k in (2, 3, 4, 5)]
                + [_resident(a.shape) for a in (n2g, nfg, emat_t, wa, wb, wo, wup, cw, cb, wdn)])
    args = [x, oa, *os_, *ls_, pm, pm, ada, ada, ada, ada, n2g, nfg, emat_t, wa, wb, wo, wup, cw, cb, wdn]
    scratch = [pltpu.VMEM((ATT_GW // LANE, tm, LANE), F32), pltpu.VMEM((1, tm, LANE), F32)]
    if per_row:
        in_specs += [rows(FFN_HIDDEN)] * 2
        args += [conv_state[:, 0][None], conv_state[:, 1][None]]
        cv_spec = rows(FFN_HIDDEN)
        cv_shape = jax.ShapeDtypeStruct((bsz, t, FFN_HIDDEN), F32)
    else:
        cv_spec = pl.BlockSpec((1, 8, FFN_HIDDEN), lambda b, i: (b, 0, 0))
        cv_shape = jax.ShapeDtypeStruct((bsz, 8, FFN_HIDDEN), F32)
        scratch.append(pltpu.VMEM((8, FFN_HIDDEN), F32))
    return pl.pallas_call(
        functools.partial(_mix_ffn_kernel, per_row_conv=per_row),
        grid=(bsz, t // tm),
        in_specs=in_specs,
        out_specs=[rows(D_MODEL), cv_spec],
        out_shape=[jax.ShapeDtypeStruct((bsz, t, D_MODEL), F32), cv_shape],
        scratch_shapes=scratch,
        compiler_params=_params(2),
        name="mix_ffn_sample" if per_row else "mix_ffn_prompt",
    )(*args)


def _t5_bucket(dist):
    max_exact = REL_BUCKETS // 2
    d = np.maximum(dist, 1).astype(np.float32)
    large = max_exact + (np.log(d / max_exact) / np.log(REL_MAX_DIST / max_exact)
                         * (REL_BUCKETS - max_exact)).astype(np.int32)
    large = np.minimum(large, REL_BUCKETS - 1)
    return np.where(dist < max_exact, dist, large).astype(np.int32)


def _prompt_bias(rel_bias_g, dil):
    i = np.arange(ATT_BLOCK)[:, None]
    j = np.arange(2 * ATT_BLOCK)[None, :]
    off = ATT_BLOCK + i - j
    valid = (off >= 0) & (off <= ATT_SPAN)
    by_off = rel_bias_g[_t5_bucket(np.arange(ATT_SPAN + 1) * dil)].T
    period = 4 * ATT_BLOCK
    seq = jnp.pad(by_off[:, ::-1], ((0, 0), (0, period - ATT_SPAN - 1)))
    rows = jnp.tile(seq, (1, ATT_BLOCK))[:, :ATT_BLOCK * (period - 1)]
    bias = rows.reshape(ATT_HPG, ATT_BLOCK, period - 1)[:, :, :2 * ATT_BLOCK]
    normal = jnp.where(valid[None], bias, NEG_BIG)
    first = jnp.where((valid & (j >= ATT_BLOCK))[None], bias, NEG_BIG)
    return jnp.stack([normal, first])


def _sample_bias(rel_bias_g, lb, dil):
    assert lb == ATT_SPAN * dil
    by_off = rel_bias_g[_t5_bucket(np.arange(ATT_SPAN + 1) * dil)].T
    strided = by_off[:, :0:-1, None]
    sb = jnp.pad(strided, ((0, 0), (0, 0), (0, dil - 1)), constant_values=NEG_BIG).reshape(ATT_HPG, lb)
    return sb, by_off[:, :1]


def _head_indicator_t():
    e = (np.arange(LANE)[:, None] == np.arange(ATT_GW)[None, :] // ATT_E).astype(np.float32)
    return jnp.asarray(e, BF16)


def _permute_w_in(w_in):
    sizes = (GLA_QK_W, GLA_QK_W, GLA_V_W, GLA_V_W, GLA_GATE_RANK, 3 * ATT_GW, 3 * ATT_GW, 3 * ATT_GW,
             D_MODEL, D_MODEL)
    offs = np.concatenate([[0], np.cumsum(sizes)])
    seg = lambda i: w_in[:, offs[i]:offs[i + 1]]
    gq, gk, gv, gr, glr, aq, ak, av, ga, gb = [seg(i) for i in range(10)]
    glr = jnp.pad(glr, ((0, 0), (0, GLR_PAD - GLA_GATE_RANK)))
    att = []
    for g in range(len(ATT_GROUPS)):
        cs = slice(g * ATT_GW, (g + 1) * ATT_GW)
        att += [aq[:, cs], ak[:, cs], av[:, cs]]
    return jnp.concatenate([gq, gk, gv, gr, ga, gb, glr] + att, axis=1).astype(BF16)


PROMPT_TM = 256
SAMPLE_TM = 128
GLA_TC = 512


def kernel(x_prompt, x_sample, state_gla, cache_win1, cache_win2, cache_win3, state_conv, c_prompt, c_sample,
           w_ada, b_ada, norm1_g, w_in, gla_wg2, gla_bg, gla_norm_g, w_branch_a, w_branch_b, w_out, rel_bias,
           norm2_g, w_up, conv_w, conv_b, w_down, normf_g):
    bp, tp, _ = x_prompt.shape
    ns = x_sample.shape[0]
    assert w_ada.shape[0] == 1, "single-layer trunk"

    w_all = _permute_w_in(w_in[0])
    wg2p = jnp.pad(gla_wg2[0], ((0, GLR_PAD - GLA_GATE_RANK), (0, 0))).astype(BF16)
    bg = gla_bg[0][None]
    gn = gla_norm_g[0][None]
    n1g, n2g, nfg = norm1_g[0][None], norm2_g[0][None], normf_g[None]
    wa, wb, wo = w_branch_a[0].astype(BF16), w_branch_b[0].astype(BF16), w_out[0].astype(BF16)
    wup, wdn = w_up[0].astype(BF16), w_down[0].astype(BF16)
    cw, cb = conv_w[0], conv_b[0][None]
    emat_t = _head_indicator_t()
    groups = range(len(ATT_GROUPS))
    rel_g = [rel_bias[:, g * ATT_HPG:(g + 1) * ATT_HPG] for g in groups]

    ada = _ada(jnp.concatenate([c_prompt, c_sample], axis=0), w_ada[0].astype(BF16), b_ada[0][None])
    ada_p = ada[:bp][:, None, :]
    ada_s = ada[bp:][None]
    xs = x_sample.reshape(1, ns, D_MODEL)

    dils = tuple(d for _, d in ATT_GROUPS)
    pm_p, *rest = _inproj(x_prompt, ada_p, n1g, w_all, PROMPT_TM, tuple(min(w, tp) for w, _ in ATT_GROUPS), dils)
    pa_p, tails_p = rest[:3], rest[3:]
    oa_p, gla_p = _gla_prompt(pm_p, wg2p, bg, gn, GLA_TC)
    os_p, ls_p = [], []
    for g in groups:
        o, l = _att_prompt(pa_p[g], _prompt_bias(rel_g[g], dils[g]), g)
        os_p.append(o)
        ls_p.append(l)
    y_p, cv_p = _mix_ffn(x_prompt, oa_p, os_p, ls_p, pm_p, ada_p, n2g, nfg, emat_t,
                         wa, wb, wo, wup, cw, cb, wdn, PROMPT_TM)

    pm_s, *rest = _inproj(xs, ada_s, n1g, w_all, SAMPLE_TM, (ns, ns, ns), (1, 1, 1))
    pa_s, new_t = rest[:3], rest[3:]
    oa_s, gla_s = _gla_sample(pm_s.reshape(ns, 1, MAIN_W), state_gla[0], wg2p, bg, gn)
    os_s, ls_s, new_caches = [], [], []
    for g, cache in enumerate((cache_win1, cache_win2, cache_win3)):
        lb = cache.shape[2]
        cache_t = jnp.transpose(cache[0], (0, 2, 3, 4, 1))
        q = pa_s[g][0, 0, :, :ATT_GW].astype(F32).reshape(ns, ATT_HPG, ATT_E)
        sb, sb0 = _sample_bias(rel_g[g], lb, dils[g])
        o, l, shifted = _att_sample(q, new_t[g][0], cache_t, sb, sb0, g)
        os_s.append(o.reshape(1, 1, ns, ATT_GW).astype(BF16))
        ls_s.append(jnp.pad(l[:, :, 0], ((0, 0), (0, LANE - ATT_HPG)))[None, None])
        new_caches.append(jnp.transpose(shifted, (0, 4, 1, 2, 3))[None])
    y_s, cv_s = _mix_ffn(xs, oa_s.reshape(1, ns, GLA_V_W), os_s, ls_s, pm_s, ada_s, n2g, nfg, emat_t,
                         wa, wb, wo, wup, cw, cb, wdn, SAMPLE_TM, conv_state=state_conv[0])

    tails = [jnp.transpose(c, (0, 4, 1, 2, 3))[None] for c in tails_p]
    conv_s = jnp.stack([state_conv[0][:, 1], cv_s[0]], axis=1)[None]
    return (y_p, y_s.reshape(ns, 1, D_MODEL), gla_p[None], gla_s[None],
            tails[0], new_caches[0], tails[1], new_caches[1], tails[2], new_caches[2],
            cv_p[:, 6:8][None], conv_s)
```

```python
import functools

import numpy as np
import jax
import jax.numpy as jnp
from jax import lax
from jax.experimental import pallas as pl
from jax.experimental.pallas import tpu as pltpu

F32 = jnp.float32
BF16 = jnp.bfloat16

D_MODEL = 1024
GLA_HEADS = 4
GLA_DK = 128
GLA_DV = 256
GLA_QK_W = GLA_HEADS * GLA_DK
GLA_V_W = GLA_HEADS * GLA_DV
GLA_GATE_RANK = 16
GLA_GATE_NORM = 16.0
GLA_CHUNK = 64
ATT_GROUPS = ((128, 1), (512, 4), (2048, 16))
ATT_HPG = 8
ATT_E = 64
ATT_GW = ATT_HPG * ATT_E
ATT_BLOCK = 128
ATT_SPAN = 128
REL_BUCKETS = 32
REL_MAX_DIST = 2048
FFN_HIDDEN = 2816
N_ADA = 6
EPS = 1e-6
NEG_BIG = -1e30

LANE = 128
GLR_PAD = LANE
MAIN_W = 2 * GLA_QK_W + 2 * GLA_V_W + 2 * D_MODEL + GLR_PAD
ATT_W3 = 3 * 3 * ATT_GW
COL_CHUNK = 512
VMEM_LIMIT = 56 * 1024 * 1024

_NT = (((1,), (1,)), ((), ()))
_TN = (((0,), (0,)), ((), ()))


def _params(n_axes):
    return pltpu.CompilerParams(dimension_semantics=("arbitrary",) * n_axes,
                                vmem_limit_bytes=VMEM_LIMIT)


def _resident(shape):
    zeros = (0,) * len(shape)
    return pl.BlockSpec(shape, lambda *_: zeros, pipeline_mode=pl.Buffered(1))


def _silu(x):
    return x * jax.nn.sigmoid(x)


def _rms(x):
    return x * lax.rsqrt(jnp.mean(x * x, axis=-1, keepdims=True) + EPS)


def _split3(x):
    a = x.astype(BF16)
    r = x - a.astype(F32)
    b = r.astype(BF16)
    c = (r - b.astype(F32)).astype(BF16)
    return a, b, c


def _dot(a, b):
    return jnp.dot(a, b, preferred_element_type=F32)


def _ada_kernel(c_ref, w_ref, b_ref, o_ref):
    s = _silu(c_ref[...]).astype(BF16)
    o_ref[...] = _dot(s, w_ref[...]) + b_ref[...]


def _ada(c_all, w_ada, b_ada):
    m = c_all.shape[0]
    n = w_ada.shape[1]
    tn = D_MODEL
    return pl.pallas_call(
        _ada_kernel,
        grid=(n // tn,),
        in_specs=[pl.BlockSpec((m, D_MODEL), lambda j: (0, 0)),
                  pl.BlockSpec((D_MODEL, tn), lambda j: (0, j)),
                  pl.BlockSpec((1, tn), lambda j: (0, j))],
        out_specs=pl.BlockSpec((m, tn), lambda j: (0, j)),
        out_shape=jax.ShapeDtypeStruct((m, n), F32),
        compiler_params=_params(1),
        name="ada",
    )(c_all, w_ada, b_ada)


def _ada_spec(ada, tm, k):
    if ada.shape[1] == 1:
        return pl.BlockSpec((1, 1, D_MODEL), lambda b, i: (b, 0, k))
    return pl.BlockSpec((1, tm, D_MODEL), lambda b, i: (b, i, k))


def _inproj_kernel(x_ref, sh_ref, sc_ref, g_ref, w_ref, pm_ref, pa1_ref, pa2_ref, pa3_ref,
                   c1_ref, c2_ref, c3_ref, stage_ref):
    tm = x_ref.shape[1]
    h = _rms(x_ref[0]) * g_ref[...]
    h = h * (1.0 + sc_ref[0]) + sh_ref[0]
    hb = h.astype(BF16)
    for c0 in range(0, MAIN_W, COL_CHUNK):
        c1 = min(c0 + COL_CHUNK, MAIN_W)
        pm_ref[0, :, c0:c1] = _dot(hb, w_ref[:, c0:c1]).astype(BF16)
    for g, (pa_ref, cache_ref) in enumerate(zip((pa1_ref, pa2_ref, pa3_ref), (c1_ref, c2_ref, c3_ref))):
        dil = pa_ref.shape[0]
        rows = cache_ref.shape[-1]
        for part in range(3):
            cs = slice(part * ATT_GW, (part + 1) * ATT_GW)
            c0 = MAIN_W + (3 * g + part) * ATT_GW
            acc = _dot(hb, w_ref[:, c0:c0 + ATT_GW])
            if dil == 1:
                pa_ref[0, :, cs] = acc.astype(BF16)
            else:
                stage = stage_ref.at[3 * g + part]
                for c in range(ATT_GW // LANE):
                    stage[c] = acc[:, c * LANE:(c + 1) * LANE]
                for r in range(dil):
                    for c in range(ATT_GW // LANE):
                        c0 = part * ATT_GW + c * LANE
                        pa_ref[r, :, c0:c0 + LANE] = stage[c, pl.ds(r, tm // dil, stride=dil), :].astype(BF16)
            if part:
                tail = acc[tm - rows:, :].T
                cache_ref[part - 1] = tail.reshape(ATT_HPG, ATT_E, rows)


def _inproj(x, ada, norm_g, w_all, tm, windows, dils):
    bsz, t, _ = x.shape
    nt = t // tm
    pa_specs, pa_shapes, cache_specs, cache_shapes = [], [], [], []
    for w, dil in zip(windows, dils):
        rows = min(tm, w)
        nblk = w // rows
        pa_specs.append(pl.BlockSpec((None, dil, tm // dil, 3 * ATT_GW), lambda b, i: (b, 0, i, 0)))
        pa_shapes.append(jax.ShapeDtypeStruct((bsz, dil, t // dil, 3 * ATT_GW), BF16))
        cache_specs.append(pl.BlockSpec(
            (None, 2, ATT_HPG, ATT_E, rows),
            functools.partial(lambda b, i, first: (b, 0, 0, 0, jnp.maximum(i - first, 0)), first=nt - nblk)))
        cache_shapes.append(jax.ShapeDtypeStruct((bsz, 2, ATT_HPG, ATT_E, w), F32))
    return pl.pallas_call(
        _inproj_kernel,
        grid=(bsz, nt),
        in_specs=[pl.BlockSpec((1, tm, D_MODEL), lambda b, i: (b, i, 0)),
                  _ada_spec(ada, tm, 0), _ada_spec(ada, tm, 1),
                  _resident((1, D_MODEL)), _resident(w_all.shape)],
        out_specs=[pl.BlockSpec((1, tm, MAIN_W), lambda b, i: (b, i, 0))] + pa_specs + cache_specs,
        out_shape=[jax.ShapeDtypeStruct((bsz, t, MAIN_W), BF16)] + pa_shapes + cache_shapes,
        scratch_shapes=[pltpu.VMEM((3 * len(windows), ATT_GW // LANE, tm, LANE), F32)],
        compiler_params=_params(2),
        name="inproj",
    )(x, ada, ada, norm_g, w_all)


def _log_sigmoid(x):
    return jnp.minimum(x, 0.0) - jnp.log1p(jnp.exp(-jnp.abs(x)))


def _gla_kernel(qk_ref, v_ref, gr_ref, glr_ref, wg2_ref, bg_ref, gn_ref, *rest):
    if len(rest) == 3:
        side_in, (o_ref, s_ref, st_ref), side_out = (), rest, ()
    else:
        side_in, (o_ref, s_ref), side_out, st_ref = rest[:5], rest[5:7], rest[7:10], rest[10]
    t = pl.program_id(1)
    c = GLA_CHUNK
    n_chunks = qk_ref.shape[1] // c

    @pl.when(t == 0)
    def _():
        st_ref[...] = jnp.zeros_like(st_ref)

    if side_in:
        _att_step(pl.program_id(0) * pl.num_programs(1) + t, *side_in, *side_out)

    row = lax.broadcasted_iota(jnp.int32, (c, c), 0)
    col = lax.broadcasted_iota(jnp.int32, (c, c), 1)
    tril = row >= col
    tri_b = jnp.where(tril, 1.0, 0.0).astype(BF16)

    def chunk(ci, carry):
        r0 = pl.multiple_of(ci * c, c)
        rows = pl.ds(r0, c)
        logits = _dot(glr_ref[0, rows, :], wg2_ref[...]) + bg_ref[...]
        g = _log_sigmoid(logits) * (1.0 / GLA_GATE_NORM)
        b = _dot_exact_lhs_t(tri_b, g)
        bref = b[c // 2:c // 2 + 1, :]
        blast = b[c - 1:c, :]
        e_in = jnp.exp(b - bref)
        e_out = jnp.exp(bref - b)
        e_b = jnp.exp(b)
        e_k = jnp.exp(blast - b)
        e_last = jnp.exp(blast)
        qk = qk_ref[0, rows, :].astype(F32)
        q = qk[:, :GLA_QK_W] * (GLA_DK ** -0.5)
        k = qk[:, GLA_QK_W:]
        v = v_ref[0, rows, :]
        gr = gr_ref[0, rows, :].astype(F32)
        for h in range(GLA_HEADS):
            ks = slice(h * GLA_DK, (h + 1) * GLA_DK)
            vs = slice(h * GLA_DV, (h + 1) * GLA_DV)
            qh, kh, vh = q[:, ks], k[:, ks], v[:, vs]
            a = lax.dot_general((qh * e_in[:, ks]).astype(BF16), (kh * e_out[:, ks]).astype(BF16),
                                _NT, preferred_element_type=F32)
            a = jnp.where(tril, a, 0.0)
            st = st_ref[h]
            o = _dot(a.astype(BF16), vh) + lax.dot_general(
                (qh * e_b[:, ks]).astype(BF16), st.astype(BF16), _NT, preferred_element_type=F32)
            kd = (kh * e_k[:, ks]).astype(BF16)
            st_ref[h] = st * e_last[:, ks] + lax.dot_general(vh, kd, _TN, preferred_element_type=F32)
            grh = gr[:, vs]
            o_ref[0, rows, vs] = (_rms(o) * gn_ref[...] * _silu(grh)).astype(BF16)
        return carry

    lax.fori_loop(0, n_chunks, chunk, 0, unroll=True)

    @pl.when(t == pl.num_programs(1) - 1)
    def _():
        for h in range(GLA_HEADS):
            s_ref[0, h] = st_ref[h].T


def _dot_exact_lhs_t(m, x):
    a, b, c = _split3(x)
    return _dot(m, a) + _dot(m, b) + _dot(m, c)


def _gla_prompt(pm, wg2p, bg, gn, tc, side=None):
    bsz, t, _ = pm.shape
    nt = t // tc
    blk = lambda k: pl.BlockSpec((1, tc, D_MODEL), functools.partial(lambda b, i, k: (b, i, k), k=k))
    in_specs = [blk(0), blk(1), blk(2),
                pl.BlockSpec((1, tc, GLR_PAD), lambda b, i: (b, i, (MAIN_W - GLR_PAD) // GLR_PAD)),
                _resident(wg2p.shape), _resident(bg.shape), _resident(gn.shape)]
    out_specs = [pl.BlockSpec((1, tc, GLA_V_W), lambda b, i: (b, i, 0)),
                 pl.BlockSpec((1, GLA_HEADS, GLA_DK, GLA_DV), lambda b, i: (b, 0, 0, 0))]
    out_shape = [jax.ShapeDtypeStruct((bsz, t, GLA_V_W), BF16),
                 jax.ShapeDtypeStruct((bsz, GLA_HEADS, GLA_DK, GLA_DV), F32)]
    args = [pm, pm, pm, pm, wg2p, bg, gn]
    if side is not None:
        q, new_t, cache_t, sb, sb0 = side
        assert cache_t.shape[0] == bsz * nt, "one sample sequence per scan step"
        s_in, s_out, s_shape = _att_sample_specs(new_t, cache_t, sb, sb0, 1, lambda b, i: b * nt + i)
        in_specs += s_in
        out_specs += s_out
        out_shape += s_shape
        args += list(side)
    return pl.pallas_call(
        _gla_kernel,
        grid=(bsz, nt),
        in_specs=in_specs,
        out_specs=out_specs,
        out_shape=out_shape,
        scratch_shapes=[pltpu.VMEM((GLA_HEADS, GLA_DV, GLA_DK), F32)],
        compiler_params=_params(2),
        name="gla_prompt",
    )(*args)


def _to_col(row):
    n = row.shape[1]
    eye = lax.broadcasted_iota(jnp.int32, (n, n), 0) == lax.broadcasted_iota(jnp.int32, (n, n), 1)
    return jnp.sum(jnp.where(eye, jnp.broadcast_to(row, (n, n)), 0.0), axis=1, keepdims=True)


def _gla_step_kernel(pm_ref, s0_ref, wg2_ref, bg_ref, gn_ref, o_ref, s_ref):
    p = pm_ref[0]
    glr = jnp.broadcast_to(p[:, MAIN_W - GLR_PAD:], (8, GLR_PAD))
    logits = _dot(glr, wg2_ref[...])[0:1] + bg_ref[...]
    eg = jnp.exp(_log_sigmoid(logits) * (1.0 / GLA_GATE_NORM))
    pf = p.astype(F32)
    q = pf[:, :GLA_QK_W] * (GLA_DK ** -0.5)
    k = pf[:, GLA_QK_W:2 * GLA_QK_W]
    v = pf[:, 2 * GLA_QK_W:2 * GLA_QK_W + GLA_V_W]
    gr = pf[:, 2 * GLA_QK_W + GLA_V_W:2 * GLA_QK_W + 2 * GLA_V_W]
    for h in range(GLA_HEADS):
        ks = slice(h * GLA_DK, (h + 1) * GLA_DK)
        vs = slice(h * GLA_DV, (h + 1) * GLA_DV)
        s_new = s0_ref[0, h] * _to_col(eg[:, ks]) + _to_col(k[:, ks]) * v[:, vs]
        s_ref[0, h] = s_new
        o = jnp.sum(_to_col(q[:, ks]) * s_new, axis=0, keepdims=True)
        o_ref[0, :, vs] = (_rms(o) * gn_ref[...] * _silu(gr[:, vs])).astype(BF16)


def _gla_sample(pm, s0, wg2p, bg, gn):
    n = pm.shape[0]
    st_spec = pl.BlockSpec((1, GLA_HEADS, GLA_DK, GLA_DV), lambda b: (b, 0, 0, 0))
    return pl.pallas_call(
        _gla_step_kernel,
        grid=(n,),
        in_specs=[pl.BlockSpec((1, 1, MAIN_W), lambda b: (b, 0, 0)), st_spec,
                  _resident(wg2p.shape), _resident(bg.shape), _resident(gn.shape)],
        out_specs=[pl.BlockSpec((1, 1, GLA_V_W), lambda b: (b, 0, 0)), st_spec],
        out_shape=[jax.ShapeDtypeStruct((n, 1, GLA_V_W), BF16),
                   jax.ShapeDtypeStruct(s0.shape, F32)],
        compiler_params=_params(1),
        name="gla_sample",
    )(pm, s0, wg2p, bg, gn)


ATT_MAX_STEP_BLOCKS = 4


def _att_kernel(q_ref, kp_ref, kc_ref, vp_ref, vc_ref, bias_ref, *rest):
    if len(rest) == 2:
        o_ref, lse_ref = rest
    else:
        o_ref, lse_ref = rest[5:7]
        step = (pl.program_id(0) * pl.num_programs(1) + pl.program_id(1)) * pl.num_programs(2) + pl.program_id(2)
        _att_step(step, *rest[:5], *rest[7:])
    kk = jnp.concatenate([kp_ref[...], kc_ref[...]], axis=0)
    vv = jnp.concatenate([vp_ref[...], vc_ref[...]], axis=0)
    lane = lax.broadcasted_iota(jnp.int32, (ATT_BLOCK, LANE), 1)
    low = lane < ATT_E
    scale = ATT_E ** -0.5
    half = (jnp.where(low, scale, 0.0).astype(BF16), jnp.where(low, 0.0, scale).astype(BF16))
    for blk in range(q_ref.shape[0] // ATT_BLOCK):
        first = jnp.where(pl.program_id(2) == 0, 1, 0) if blk == 0 else 0
        rows = slice(blk * ATT_BLOCK, (blk + 1) * ATT_BLOCK)
        keys = slice(blk * ATT_BLOCK, (blk + 2) * ATT_BLOCK)
        lse_tile = jnp.zeros((ATT_BLOCK, LANE), F32)
        for p in range(ATT_HPG // 2):
            cs = slice(p * LANE, (p + 1) * LANE)
            qp, kp, vp = q_ref[rows, cs], kk[keys, cs], vv[keys, cs]
            outs = []
            for hh in range(2):
                h = 2 * p + hh
                s = lax.dot_general(qp * half[hh], kp, _NT, preferred_element_type=F32)
                s = s + bias_ref[first, h]
                m = jnp.max(s, axis=-1, keepdims=True)
                e = jnp.exp(s - m)
                den = jnp.sum(e, axis=-1, keepdims=True)
                outs.append(_dot(e.astype(BF16), vp) / den)
                lse_tile = jnp.where(lane == h, m + jnp.log(den), lse_tile)
            o_ref[rows, cs] = jnp.where(low, outs[0], outs[1]).astype(BF16)
        lse_ref[rows, :] = lse_tile


def _att_steps(pa):
    bsz, dil, l, _ = pa.shape
    step_blocks = min(ATT_MAX_STEP_BLOCKS, l // ATT_BLOCK)
    return (bsz, dil, l // (step_blocks * ATT_BLOCK)), step_blocks


def _att_prompt(pa, bias2, g, side=None):
    bsz, dil, l, _ = pa.shape
    grid, step_blocks = _att_steps(pa)
    step_rows = step_blocks * ATT_BLOCK
    own = lambda part: pl.BlockSpec((None, None, step_rows, ATT_GW),
                                    functools.partial(lambda b, r, n, part: (b, r, n, part), part=part))
    prev = lambda part: pl.BlockSpec(
        (None, None, ATT_BLOCK, ATT_GW),
        functools.partial(lambda b, r, n, part: (b, r, jnp.maximum(step_blocks * n - 1, 0), part), part=part))
    out_spec = lambda width: pl.BlockSpec((None, None, step_rows, width), lambda b, r, n: (b, r, n, 0))
    in_specs = [own(0), prev(1), own(1), prev(2), own(2), _resident(bias2.shape)]
    out_specs = [out_spec(ATT_GW), out_spec(LANE)]
    out_shape = [jax.ShapeDtypeStruct((bsz, dil, l, ATT_GW), BF16),
                 jax.ShapeDtypeStruct((bsz, dil, l, LANE), F32)]
    args = [pa, pa, pa, pa, pa, bias2]
    if side is not None:
        new_t, cache_t, sb, sb0 = side[1:]
        assert cache_t.shape[0] == np.prod(grid), "one sample sequence per attention step"
        s_in, s_out, s_shape = _att_sample_specs(new_t, cache_t, sb, sb0, 1,
                                                 lambda b, r, n: (b * grid[1] + r) * grid[2] + n)
        in_specs += s_in
        out_specs += s_out
        out_shape += s_shape
        args += list(side)
    return pl.pallas_call(
        _att_kernel,
        grid=grid,
        in_specs=in_specs,
        out_specs=out_specs,
        out_shape=out_shape,
        compiler_params=_params(3),
        name=f"att_prompt_g{g}",
    )(*args)


def _att_step_kernel(q_ref, new_ref, buf_ref, sb_ref, sb0_ref, o_ref, l_ref, out_ref):
    _att_step(pl.program_id(0) * buf_ref.shape[0], q_ref, new_ref, buf_ref, sb_ref, sb0_ref, o_ref, l_ref, out_ref)


def _att_step(first_seq, q_ref, new_ref, buf_ref, sb_ref, sb0_ref, o_ref, l_ref, out_ref):
    seqs, _, _, _, lb = buf_ref.shape
    n = new_ref.shape[-1]
    scale = ATT_E ** -0.5
    head = lax.broadcasted_iota(jnp.int32, (ATT_HPG, 1), 0)
    last = lax.broadcasted_iota(jnp.int32, (1, lb), 1) == lb - 1
    for j in range(seqs):
        b = first_seq + j
        mine = lax.broadcasted_iota(jnp.int32, (1, n), 1) == b
        qb = q_ref[j].astype(BF16)
        s = jnp.zeros((ATT_HPG, lb), F32)
        sn = jnp.zeros((ATT_HPG, n), F32)
        for h in range(ATT_HPG):
            s = jnp.where(head == h, _dot(qb, buf_ref[j, 0, h].astype(BF16)), s)
            sn = jnp.where(head == h, _dot(qb, new_ref[0, h].astype(BF16)), sn)
        s = s * scale + sb_ref[...]
        sn = jnp.where(mine, sn * scale + sb0_ref[...], NEG_BIG)
        m = jnp.maximum(jnp.max(s, axis=-1, keepdims=True), jnp.max(sn, axis=-1, keepdims=True))
        e = jnp.exp(s - m)
        en = jnp.exp(sn - m)
        den = jnp.sum(e, axis=-1, keepdims=True) + jnp.sum(en, axis=-1, keepdims=True)
        pb = (e / den).astype(BF16)
        pnb = (en / den).astype(BF16)
        o = jnp.zeros((ATT_HPG, ATT_E), F32)
        for h in range(ATT_HPG):
            oh = (lax.dot_general(pb, buf_ref[j, 1, h].astype(BF16), _NT, preferred_element_type=F32)
                  + lax.dot_general(pnb, new_ref[1, h].astype(BF16), _NT, preferred_element_type=F32))
            o = jnp.where(head == h, oh, o)
        o_ref[j] = o
        l_ref[j] = jnp.broadcast_to(m + jnp.log(den), (ATT_HPG, LANE))
        for kv in range(2):
            for h in range(ATT_HPG):
                col = jnp.sum(jnp.where(mine, new_ref[kv, h], 0.0), axis=-1, keepdims=True)
                out_ref[j, kv, h] = jnp.where(last, col, pltpu.roll(buf_ref[j, kv, h], lb - 1, 1))


ATT_SAMPLE_STEP_BYTES = 8 * 1024 * 1024


def _att_sample(q, new_t, cache_t, sb, sb0, g):
    n, _, _, _, lb = cache_t.shape
    seqs = max(1, min(8, ATT_SAMPLE_STEP_BYTES // (2 * ATT_GW * lb * 4)))
    in_specs, out_specs, out_shape = _att_sample_specs(new_t, cache_t, sb, sb0, seqs, lambda i: i)
    return pl.pallas_call(
        _att_step_kernel,
        grid=(n // seqs,),
        in_specs=in_specs,
        out_specs=out_specs,
        out_shape=out_shape,
        compiler_params=_params(1),
        name=f"att_sample_g{g}",
    )(q, new_t, cache_t, sb, sb0)


def _att_sample_specs(new_t, cache_t, sb, sb0, seqs, block_of):
    n, _, _, _, lb = cache_t.shape
    planes = pl.BlockSpec((seqs, 2, ATT_HPG, ATT_E, lb), lambda *ids: (block_of(*ids), 0, 0, 0, 0))
    per_seq = lambda width: pl.BlockSpec((seqs, ATT_HPG, width), lambda *ids: (block_of(*ids), 0, 0))
    in_specs = [per_seq(ATT_E), _resident(new_t.shape), planes, _resident(sb.shape), _resident(sb0.shape)]
    out_specs = [per_seq(ATT_E), per_seq(LANE), planes]
    out_shape = [jax.ShapeDtypeStruct((n, ATT_HPG, ATT_E), F32),
                 jax.ShapeDtypeStruct((n, ATT_HPG, LANE), F32),
                 jax.ShapeDtypeStruct(cache_t.shape, F32)]
    return in_specs, out_specs, out_shape


def _mix_ffn_kernel(*refs, per_row_conv):
    (x_ref, oa_ref, o1_ref, o2_ref, o3_ref, l1_ref, l2_ref, l3_ref, ga_ref, gb_ref,
     g1_ref, sh2_ref, sc2_ref, g2_ref, n2_ref, nf_ref, et_ref,
     wa_ref, wb_ref, wo_ref, wup_ref, cw_ref, cb_ref, wdn_ref) = refs[:24]
    if per_row_conv:
        cs0_ref, cs1_ref, y_ref, cv_ref, so_ref, sl_ref = refs[24:]
    else:
        y_ref, cv_ref, so_ref, sl_ref, carry_ref = refs[24:]

        @pl.when(pl.program_id(1) == 0)
        def _():
            carry_ref[...] = jnp.zeros_like(carry_ref)
    tm = x_ref.shape[1]

    def token_order(ref, stage_ref):
        dil, _, width = ref.shape
        if dil == 1:
            return ref[0].astype(F32)
        for r in range(dil):
            plane = ref[r].astype(F32)
            for c in range(width // LANE):
                stage_ref[c, pl.ds(r, tm // dil, stride=dil), :] = plane[:, c * LANE:(c + 1) * LANE]
        return jnp.concatenate([stage_ref[c] for c in range(width // LANE)], axis=1)

    ls = [token_order(l_ref, sl_ref) for l_ref in (l1_ref, l2_ref, l3_ref)]
    m = jnp.maximum(jnp.maximum(ls[0], ls[1]), ls[2])
    ws = [jnp.exp(l - m) for l in ls]
    den = ws[0] + ws[1] + ws[2]
    o_att = jnp.zeros((tm, ATT_GW), F32)
    lane = lax.broadcasted_iota(jnp.int32, (tm, LANE), 1)
    for w, o_ref in zip(ws, (o1_ref, o2_ref, o3_ref)):
        wn = w / den
        hi = wn.astype(BF16).astype(F32)
        packed = jnp.where(lane < ATT_HPG, hi, pltpu.roll(wn - hi, ATT_HPG, 1)).astype(BF16)
        o_att = o_att + _dot(packed, et_ref[...]) * token_order(o_ref, so_ref)

    y_a = _dot(oa_ref[0], wa_ref[...])
    y_b = _dot(o_att.astype(BF16), wb_ref[...])
    mix_in = jax.nn.sigmoid(ga_ref[0].astype(F32)) * y_a + jax.nn.sigmoid(gb_ref[0].astype(F32)) * y_b
    x1 = x_ref[0] + g1_ref[0] * _dot(mix_in.astype(BF16), wo_ref[...])

    h2 = (_rms(x1) * n2_ref[...] * (1.0 + sc2_ref[0]) + sh2_ref[0]).astype(BF16)
    a = _dot(h2, wup_ref[:, :FFN_HIDDEN])
    bgate = _dot(h2, wup_ref[:, FFN_HIDDEN:])
    if per_row_conv:
        a2, a1 = cs0_ref[0], cs1_ref[0]
        cv_ref[0] = a
    else:
        row = lax.broadcasted_iota(jnp.int32, (tm, FFN_HIDDEN), 0)
        prev = carry_ref[...]
        a1 = jnp.where(row == 0, prev[7:8], pltpu.roll(a, 1, 0))
        a2 = jnp.where(row == 0, prev[6:7], jnp.where(row == 1, prev[7:8], pltpu.roll(a, 2, 0)))
        carry_ref[...] = a[tm - 8:, :]
        cv_ref[0] = a[tm - 8:, :]
    a_conv = cb_ref[...] + (cw_ref[0:1] * a2 + cw_ref[1:2] * a1 + cw_ref[2:3] * a)
    y_f = _dot((_silu(a_conv) * bgate).astype(BF16), wdn_ref[...])
    x2 = x1 + g2_ref[0] * y_f
    y_ref[0] = _rms(x2) * nf_ref[...]


def _mix_ffn(x, oa, os_, ls_, pm, ada, n2g, nfg, emat_t, wa, wb, wo, wup, cw, cb, wdn, tm, conv_state=None):
    bsz, t, _ = x.shape
    per_row = conv_state is not None
    rows = lambda width, k=0: pl.BlockSpec(
        (1, tm, width), functools.partial(lambda b, i, k: (b, i, k), k=k))
    planes = lambda a: pl.BlockSpec((None, a.shape[1], tm // a.shape[1], a.shape[3]),
                                    lambda b, i: (b, 0, i, 0))
    in_specs = ([rows(D_MODEL), rows(GLA_V_W)] + [planes(a) for a in (*os_, *ls_)]
                + [rows(D_MODEL, 3), rows(D_MODEL, 4)]
                + [_ada_spec(ada, tm, k) for k in (2, 3, 4, 5)]
                + [_resident(a.shape) for a in (n2g, nfg, emat_t, wa, wb, wo, wup, cw, cb, wdn)])
    args = [x, oa, *os_, *ls_, pm, pm, ada, ada, ada, ada, n2g, nfg, emat_t, wa, wb, wo, wup, cw, cb, wdn]
    scratch = [pltpu.VMEM((ATT_GW // LANE, tm, LANE), F32), pltpu.VMEM((1, tm, LANE), F32)]
    if per_row:
        in_specs += [rows(FFN_HIDDEN)] * 2
        args += [conv_state[:, 0][None], conv_state[:, 1][None]]
        cv_spec = rows(FFN_HIDDEN)
        cv_shape = jax.ShapeDtypeStruct((bsz, t, FFN_HIDDEN), F32)
    else:
        cv_spec = pl.BlockSpec((1, 8, FFN_HIDDEN), lambda b, i: (b, 0, 0))
        cv_shape = jax.ShapeDtypeStruct((bsz, 8, FFN_HIDDEN), F32)
        scratch.append(pltpu.VMEM((8, FFN_HIDDEN), F32))
    return pl.pallas_call(
        functools.partial(_mix_ffn_kernel, per_row_conv=per_row),
        grid=(bsz, t // tm),
        in_specs=in_specs,
        out_specs=[rows(D_MODEL), cv_spec],
        out_shape=[jax.ShapeDtypeStruct((bsz, t, D_MODEL), F32), cv_shape],
        scratch_shapes=scratch,
        compiler_params=_params(2),
        name="mix_ffn_sample" if per_row else "mix_ffn_prompt",
    )(*args)


def _t5_bucket(dist):
    max_exact = REL_BUCKETS // 2
    d = np.maximum(dist, 1).astype(np.float32)
    large = max_exact + (np.log(d / max_exact) / np.log(REL_MAX_DIST / max_exact)
                         * (REL_BUCKETS - max_exact)).astype(np.int32)
    large = np.minimum(large, REL_BUCKETS - 1)
    return np.where(dist < max_exact, dist, large).astype(np.int32)


def _prompt_bias(rel_bias_g, dil):
    i = np.arange(ATT_BLOCK)[:, None]
    j = np.arange(2 * ATT_BLOCK)[None, :]
    off = ATT_BLOCK + i - j
    valid = (off >= 0) & (off <= ATT_SPAN)
    by_off = rel_bias_g[_t5_bucket(np.arange(ATT_SPAN + 1) * dil)].T
    period = 4 * ATT_BLOCK
    seq = jnp.pad(by_off[:, ::-1], ((0, 0), (0, period - ATT_SPAN - 1)))
    rows = jnp.tile(seq, (1, ATT_BLOCK))[:, :ATT_BLOCK * (period - 1)]
    bias = rows.reshape(ATT_HPG, ATT_BLOCK, period - 1)[:, :, :2 * ATT_BLOCK]
    normal = jnp.where(valid[None], bias, NEG_BIG)
    first = jnp.where((valid & (j >= ATT_BLOCK))[None], bias, NEG_BIG)
    return jnp.stack([normal, first])


def _sample_bias(rel_bias_g, lb, dil):
    assert lb == ATT_SPAN * dil
    by_off = rel_bias_g[_t5_bucket(np.arange(ATT_SPAN + 1) * dil)].T
    strided = by_off[:, :0:-1, None]
    sb = jnp.pad(strided, ((0, 0), (0, 0), (0, dil - 1)), constant_values=NEG_BIG).reshape(ATT_HPG, lb)
    return sb, by_off[:, :1]


def _head_indicator_t():
    r = np.arange(LANE)[:, None]
    e = ((r < 2 * ATT_HPG) & (r % ATT_HPG == np.arange(ATT_GW)[None, :] // ATT_E)).astype(np.float32)
    return jnp.asarray(e, BF16)


def _permute_w_in(w_in):
    sizes = (GLA_QK_W, GLA_QK_W, GLA_V_W, GLA_V_W, GLA_GATE_RANK, 3 * ATT_GW, 3 * ATT_GW, 3 * ATT_GW,
             D_MODEL, D_MODEL)
    offs = np.concatenate([[0], np.cumsum(sizes)])
    seg = lambda i: w_in[:, offs[i]:offs[i + 1]]
    gq, gk, gv, gr, glr, aq, ak, av, ga, gb = [seg(i) for i in range(10)]
    glr = jnp.pad(glr, ((0, 0), (0, GLR_PAD - GLA_GATE_RANK)))
    att = []
    for g in range(len(ATT_GROUPS)):
        cs = slice(g * ATT_GW, (g + 1) * ATT_GW)
        att += [aq[:, cs], ak[:, cs], av[:, cs]]
    return jnp.concatenate([gq, gk, gv, gr, ga, gb, glr] + att, axis=1).astype(BF16)


PROMPT_TM = 256
SAMPLE_TM = 128
GLA_TC = 512


def kernel(x_prompt, x_sample, state_gla, cache_win1, cache_win2, cache_win3, state_conv, c_prompt, c_sample,
           w_ada, b_ada, norm1_g, w_in, gla_wg2, gla_bg, gla_norm_g, w_branch_a, w_branch_b, w_out, rel_bias,
           norm2_g, w_up, conv_w, conv_b, w_down, normf_g):
    bp, tp, _ = x_prompt.shape
    ns = x_sample.shape[0]
    assert w_ada.shape[0] == 1, "single-layer trunk"

    w_all = _permute_w_in(w_in[0])
    wg2p = jnp.pad(gla_wg2[0], ((0, GLR_PAD - GLA_GATE_RANK), (0, 0))).astype(BF16)
    bg = gla_bg[0][None]
    gn = gla_norm_g[0][None]
    n1g, n2g, nfg = norm1_g[0][None], norm2_g[0][None], normf_g[None]
    wa, wb, wo = w_branch_a[0].astype(BF16), w_branch_b[0].astype(BF16), w_out[0].astype(BF16)
    wup, wdn = w_up[0].astype(BF16), w_down[0].astype(BF16)
    cw, cb = conv_w[0], conv_b[0][None]
    emat_t = _head_indicator_t()
    groups = range(len(ATT_GROUPS))
    rel_g = [rel_bias[:, g * ATT_HPG:(g + 1) * ATT_HPG] for g in groups]

    ada = _ada(jnp.concatenate([c_prompt, c_sample], axis=0), w_ada[0].astype(BF16), b_ada[0][None])
    ada_p = ada[:bp][:, None, :]
    ada_s = ada[bp:][None]
    xs = x_sample.reshape(1, ns, D_MODEL)

    dils = tuple(d for _, d in ATT_GROUPS)
    pm_p, *rest = _inproj(x_prompt, ada_p, n1g, w_all, PROMPT_TM, tuple(min(w, tp) for w, _ in ATT_GROUPS), dils)
    pa_p, tails_p = rest[:3], rest[3:]
    pm_s, *rest = _inproj(xs, ada_s, n1g, w_all, SAMPLE_TM, (ns, ns, ns), (1, 1, 1))
    pa_s, new_t = rest[:3], rest[3:]

    sample_att = []
    for g, cache in enumerate((cache_win1, cache_win2, cache_win3)):
        cache_t = jnp.transpose(cache[0], (0, 2, 3, 4, 1))
        q = pa_s[g][0, 0, :, :ATT_GW].astype(F32).reshape(ns, ATT_HPG, ATT_E)
        sample_att.append((q, new_t[g][0], cache_t) + _sample_bias(rel_g[g], cache.shape[2], dils[g]))
    hosts = {"gla": bp * (tp // GLA_TC)}
    hosts.update({g: int(np.prod(_att_steps(pa_p[g])[0])) for g in groups})
    rider = {}
    for sg in reversed(groups):
        host = next((h for h, steps in hosts.items() if steps == ns and h not in rider), None)
        if host is not None:
            rider[host] = sg
    res_s = [None if g in rider.values() else _att_sample(*sample_att[g], g) for g in groups]
    side = lambda host: sample_att[rider[host]] if host in rider else None

    oa_p, gla_p, *rode = _gla_prompt(pm_p, wg2p, bg, gn, GLA_TC, side("gla"))
    if rode:
        res_s[rider["gla"]] = rode
    os_p, ls_p = [], []
    for g in groups:
        o, l, *rode = _att_prompt(pa_p[g], _prompt_bias(rel_g[g], dils[g]), g, side(g))
        if rode:
            res_s[rider[g]] = rode
        os_p.append(o)
        ls_p.append(l)
    y_p, cv_p = _mix_ffn(x_prompt, oa_p, os_p, ls_p, pm_p, ada_p, n2g, nfg, emat_t,
                         wa, wb, wo, wup, cw, cb, wdn, PROMPT_TM)

    oa_s, gla_s = _gla_sample(pm_s.reshape(ns, 1, MAIN_W), state_gla[0], wg2p, bg, gn)
    os_s, ls_s, new_caches = [], [], []
    for o, l, shifted in res_s:
        os_s.append(o.reshape(1, 1, ns, ATT_GW).astype(BF16))
        ls_s.append(jnp.pad(l[:, :, 0], ((0, 0), (0, LANE - ATT_HPG)))[None, None])
        new_caches.append(jnp.transpose(shifted, (0, 4, 1, 2, 3))[None])
    y_s, cv_s = _mix_ffn(xs, oa_s.reshape(1, ns, GLA_V_W), os_s, ls_s, pm_s, ada_s, n2g, nfg, emat_t,
                         wa, wb, wo, wup, cw, cb, wdn, SAMPLE_TM, conv_state=state_conv[0])

    tails = [jnp.transpose(c, (0, 4, 1, 2, 3))[None] for c in tails_p]
    conv_s = jnp.stack([state_conv[0][:, 1], cv_s[0]], axis=1)[None]
    return (y_p, y_s.reshape(ns, 1, D_MODEL), gla_p[None], gla_s[None],
            tails[0], new_caches[0], tails[1], new_caches[1], tails[2], new_caches[2],
            cv_p[:, 6:8][None], conv_s)
```

```python
import functools

import numpy as np
import jax
import jax.numpy as jnp
from jax import lax
from jax.experimental import pallas as pl
from jax.experimental.pallas import tpu as pltpu

F32 = jnp.float32
BF16 = jnp.bfloat16

D_MODEL = 1024
GLA_HEADS = 4
GLA_DK = 128
GLA_DV = 256
GLA_QK_W = GLA_HEADS * GLA_DK
GLA_V_W = GLA_HEADS * GLA_DV
GLA_GATE_RANK = 16
GLA_GATE_NORM = 16.0
GLA_CHUNK = 64
ATT_GROUPS = ((128, 1), (512, 4), (2048, 16))
ATT_HPG = 8
ATT_E = 64
ATT_GW = ATT_HPG * ATT_E
ATT_BLOCK = 128
ATT_SPAN = 128
REL_BUCKETS = 32
REL_MAX_DIST = 2048
FFN_HIDDEN = 2816
N_ADA = 6
EPS = 1e-6
NEG_BIG = -1e30

LANE = 128
GLR_PAD = LANE
MAIN_W = 2 * GLA_QK_W + 2 * GLA_V_W + 2 * D_MODEL + GLR_PAD
ATT_W3 = 3 * 3 * ATT_GW
W_ATT0 = 2 * GLA_QK_W + 2 * GLA_V_W
COL_CHUNK = 512
VMEM_LIMIT = 56 * 1024 * 1024

_NT = (((1,), (1,)), ((), ()))
_TN = (((0,), (0,)), ((), ()))


def _params(n_axes):
    return pltpu.CompilerParams(dimension_semantics=("arbitrary",) * n_axes,
                                vmem_limit_bytes=VMEM_LIMIT)


def _resident(shape):
    zeros = (0,) * len(shape)
    return pl.BlockSpec(shape, lambda *_: zeros, pipeline_mode=pl.Buffered(1))


def _silu(x):
    return x * jax.nn.sigmoid(x)


def _rms(x):
    return x * lax.rsqrt(jnp.mean(x * x, axis=-1, keepdims=True) + EPS)


def _split3(x):
    a = x.astype(BF16)
    r = x - a.astype(F32)
    b = r.astype(BF16)
    c = (r - b.astype(F32)).astype(BF16)
    return a, b, c


def _dot(a, b):
    return jnp.dot(a, b, preferred_element_type=F32)


def _ada_kernel(c_ref, w_ref, b_ref, o_ref):
    s = _silu(c_ref[...]).astype(BF16)
    o_ref[...] = _dot(s, w_ref[...]) + b_ref[...]


def _ada(c_all, w_ada, b_ada):
    m = c_all.shape[0]
    n = w_ada.shape[1]
    tn = D_MODEL
    return pl.pallas_call(
        _ada_kernel,
        grid=(n // tn,),
        in_specs=[pl.BlockSpec((m, D_MODEL), lambda j: (0, 0)),
                  pl.BlockSpec((D_MODEL, tn), lambda j: (0, j)),
                  pl.BlockSpec((1, tn), lambda j: (0, j))],
        out_specs=pl.BlockSpec((m, tn), lambda j: (0, j)),
        out_shape=jax.ShapeDtypeStruct((m, n), F32),
        compiler_params=_params(1),
        name="ada",
    )(c_all, w_ada, b_ada)


def _ada_spec(ada, tm, k):
    if ada.shape[1] == 1:
        return pl.BlockSpec((1, 1, D_MODEL), lambda b, i: (b, 0, k))
    return pl.BlockSpec((1, tm, D_MODEL), lambda b, i: (b, i, k))


def _inproj_kernel(x_ref, sh_ref, sc_ref, g_ref, w_ref, pm_ref, pa1_ref, pa2_ref, pa3_ref,
                   c1_ref, c2_ref, c3_ref, stage_ref):
    tm = x_ref.shape[1]
    h = _rms(x_ref[0]) * g_ref[...]
    h = h * (1.0 + sc_ref[0]) + sh_ref[0]
    hb = h.astype(BF16)
    for c0 in range(0, MAIN_W, COL_CHUNK):
        c1 = min(c0 + COL_CHUNK, MAIN_W)
        w0 = c0 if c0 < W_ATT0 else c0 + ATT_W3
        pm_ref[0, :, c0:c1] = _dot(hb, w_ref[:, w0:w0 + c1 - c0]).astype(BF16)
    for g, (pa_ref, cache_ref) in enumerate(zip((pa1_ref, pa2_ref, pa3_ref), (c1_ref, c2_ref, c3_ref))):
        dil = pa_ref.shape[0]
        rows = cache_ref.shape[-1]
        for part in range(3):
            cs = slice(part * ATT_GW, (part + 1) * ATT_GW)
            c0 = W_ATT0 + part * 3 * ATT_GW + g * ATT_GW
            acc = _dot(hb, w_ref[:, c0:c0 + ATT_GW])
            if dil == 1:
                pa_ref[0, :, cs] = acc.astype(BF16)
            else:
                stage = stage_ref.at[3 * g + part]
                for c in range(ATT_GW // LANE):
                    stage[c] = acc[:, c * LANE:(c + 1) * LANE]
                for r in range(dil):
                    for c in range(ATT_GW // LANE):
                        c0 = part * ATT_GW + c * LANE
                        pa_ref[r, :, c0:c0 + LANE] = stage[c, pl.ds(r, tm // dil, stride=dil), :].astype(BF16)
            if part:
                tail = acc[tm - rows:, :].T
                cache_ref[part - 1] = tail.reshape(ATT_HPG, ATT_E, rows)


def _inproj(x, ada, norm_g, w_all, tm, windows, dils):
    bsz, t, _ = x.shape
    nt = t // tm
    pa_specs, pa_shapes, cache_specs, cache_shapes = [], [], [], []
    for w, dil in zip(windows, dils):
        rows = min(tm, w)
        nblk = w // rows
        pa_specs.append(pl.BlockSpec((None, dil, tm // dil, 3 * ATT_GW), lambda b, i: (b, 0, i, 0)))
        pa_shapes.append(jax.ShapeDtypeStruct((bsz, dil, t // dil, 3 * ATT_GW), BF16))
        cache_specs.append(pl.BlockSpec(
            (None, 2, ATT_HPG, ATT_E, rows),
            functools.partial(lambda b, i, first: (b, 0, 0, 0, jnp.maximum(i - first, 0)), first=nt - nblk)))
        cache_shapes.append(jax.ShapeDtypeStruct((bsz, 2, ATT_HPG, ATT_E, w), F32))
    return pl.pallas_call(
        _inproj_kernel,
        grid=(bsz, nt),
        in_specs=[pl.BlockSpec((1, tm, D_MODEL), lambda b, i: (b, i, 0)),
                  _ada_spec(ada, tm, 0), _ada_spec(ada, tm, 1),
                  _resident((1, D_MODEL)), _resident(w_all.shape)],
        out_specs=[pl.BlockSpec((1, tm, MAIN_W), lambda b, i: (b, i, 0))] + pa_specs + cache_specs,
        out_shape=[jax.ShapeDtypeStruct((bsz, t, MAIN_W), BF16)] + pa_shapes + cache_shapes,
        scratch_shapes=[pltpu.VMEM((3 * len(windows), ATT_GW // LANE, tm, LANE), F32)],
        compiler_params=_params(2),
        name="inproj",
    )(x, ada, ada, norm_g, w_all)


def _log_sigmoid(x):
    return jnp.minimum(x, 0.0) - jnp.log1p(jnp.exp(-jnp.abs(x)))


def _gla_kernel(qk_ref, v_ref, gr_ref, glr_ref, wg2_ref, bg_ref, gn_ref, tri_ref, *rest):
    scratch = rest[-7:]
    rest = rest[:-7]
    if len(rest) == 2:
        side_in, (o_ref, s_ref), side_out = (), rest, ()
    else:
        side_in, (o_ref, s_ref), side_out = rest[:5], rest[5:7], rest[7:10]
    st_ref, qi_ref, ki_ref, qb_ref, kd_ref, oraw_ref, u_ref = scratch
    t = pl.program_id(1)
    c = GLA_CHUNK
    tc = qk_ref.shape[1]
    n_chunks = tc // c

    @pl.when(t == 0)
    def _():
        st_ref[...] = jnp.zeros_like(st_ref)

    if side_in:
        _att_step(pl.program_id(0) * pl.num_programs(1) + t, *side_in, *side_out)

    logits = _dot(glr_ref[0], wg2_ref[...]) + bg_ref[...]
    g = _log_sigmoid(logits) * (1.0 / GLA_GATE_NORM)
    b = _dot_exact_lhs_t(tri_ref[...], g).reshape(n_chunks, c, GLA_QK_W)
    bref = b[:, c // 2:c // 2 + 1, :]
    blast = b[:, c - 1:c, :]
    qk = qk_ref[0].astype(F32).reshape(n_chunks, c, 2 * GLA_QK_W)
    q = qk[:, :, :GLA_QK_W] * (GLA_DK ** -0.5)
    k = qk[:, :, GLA_QK_W:]
    qi_ref[...] = (q * jnp.exp(b - bref)).astype(BF16)
    ki_ref[...] = (k * jnp.exp(bref - b)).astype(BF16)
    qb_ref[...] = (q * jnp.exp(b)).astype(BF16)
    kd_ref[...] = (k * jnp.exp(blast - b)).astype(BF16)
    e_last = jnp.exp(blast)

    row = lax.broadcasted_iota(jnp.int32, (c, c), 0)
    col = lax.broadcasted_iota(jnp.int32, (c, c), 1)
    tril = row >= col
    for ci in range(n_chunks):
        rows = slice(ci * c, (ci + 1) * c)
        for h in range(GLA_HEADS):
            ks = slice(h * GLA_DK, (h + 1) * GLA_DK)
            vs = slice(h * GLA_DV, (h + 1) * GLA_DV)
            vh = v_ref[0, rows, vs]
            a = lax.dot_general(qi_ref[ci, :, ks], ki_ref[ci, :, ks], _NT, preferred_element_type=F32)
            a = jnp.where(tril, a, 0.0)
            oraw_ref[rows, vs] = _dot(a.astype(BF16), vh)
            u_ref[ci, h] = lax.dot_general(vh, kd_ref[ci, :, ks], _TN, preferred_element_type=F32)
    for ci in range(n_chunks):
        rows = slice(ci * c, (ci + 1) * c)
        for h in range(GLA_HEADS):
            ks = slice(h * GLA_DK, (h + 1) * GLA_DK)
            vs = slice(h * GLA_DV, (h + 1) * GLA_DV)
            st = st_ref[h]
            oraw_ref[rows, vs] += lax.dot_general(
                qb_ref[ci, :, ks], st.astype(BF16), _NT, preferred_element_type=F32)
            st_ref[h] = st * e_last[ci, :, ks] + u_ref[ci, h]

    for h in range(GLA_HEADS):
        vs = slice(h * GLA_DV, (h + 1) * GLA_DV)
        o_ref[0, :, vs] = (_rms(oraw_ref[:, vs]) * gn_ref[...]
                           * _silu(gr_ref[0, :, vs].astype(F32))).astype(BF16)

    @pl.when(t == pl.num_programs(1) - 1)
    def _():
        for h in range(GLA_HEADS):
            s_ref[0, h] = st_ref[h].T


def _dot_exact_lhs_t(m, x):
    a, b, c = _split3(x)
    return _dot(m, a) + _dot(m, b) + _dot(m, c)


def _gla_prompt(pm, wg2p, bg, gn, tc, side=None):
    bsz, t, _ = pm.shape
    nt = t // tc
    blk = lambda k: pl.BlockSpec((1, tc, D_MODEL), functools.partial(lambda b, i, k: (b, i, k), k=k))
    pos = np.arange(tc)
    same_chunk = pos[:, None] // GLA_CHUNK == pos[None, :] // GLA_CHUNK
    tri = jnp.asarray(same_chunk & (pos[:, None] >= pos[None, :]), BF16)
    in_specs = [blk(0), blk(1), blk(2),
                pl.BlockSpec((1, tc, GLR_PAD), lambda b, i: (b, i, (MAIN_W - GLR_PAD) // GLR_PAD)),
                _resident(wg2p.shape), _resident(bg.shape), _resident(gn.shape), _resident(tri.shape)]
    out_specs = [pl.BlockSpec((1, tc, GLA_V_W), lambda b, i: (b, i, 0)),
                 pl.BlockSpec((1, GLA_HEADS, GLA_DK, GLA_DV), lambda b, i: (b, 0, 0, 0))]
    out_shape = [jax.ShapeDtypeStruct((bsz, t, GLA_V_W), BF16),
                 jax.ShapeDtypeStruct((bsz, GLA_HEADS, GLA_DK, GLA_DV), F32)]
    args = [pm, pm, pm, pm, wg2p, bg, gn, tri]
    staged = pltpu.VMEM((tc // GLA_CHUNK, GLA_CHUNK, GLA_QK_W), BF16)
    if side is not None:
        q, new_t, cache_t, sb, sb0 = side
        assert cache_t.shape[0] == bsz * nt, "one sample sequence per scan step"
        s_in, s_out, s_shape = _att_sample_specs(new_t, cache_t, sb, sb0, 1, lambda b, i: b * nt + i)
        in_specs += s_in
        out_specs += s_out
        out_shape += s_shape
        args += list(side)
    return pl.pallas_call(
        _gla_kernel,
        grid=(bsz, nt),
        in_specs=in_specs,
        out_specs=out_specs,
        out_shape=out_shape,
        scratch_shapes=[pltpu.VMEM((GLA_HEADS, GLA_DV, GLA_DK), F32), staged, staged, staged, staged,
                        pltpu.VMEM((tc, GLA_V_W), F32),
                        pltpu.VMEM((tc // GLA_CHUNK, GLA_HEADS, GLA_DV, GLA_DK), F32)],
        compiler_params=_params(2),
        name="gla_prompt",
    )(*args)


def _to_col(row):
    n = row.shape[1]
    eye = lax.broadcasted_iota(jnp.int32, (n, n), 0) == lax.broadcasted_iota(jnp.int32, (n, n), 1)
    return jnp.sum(jnp.where(eye, jnp.broadcast_to(row, (n, n)), 0.0), axis=1, keepdims=True)


GLA_SAMPLE_SEQS = 8


def _gla_step_kernel(pm_ref, s0_ref, wg2_ref, bg_ref, gn_ref, o_ref, s_ref):
    for j in range(pm_ref.shape[0]):
        p = pm_ref[j]
        glr = jnp.broadcast_to(p[:, MAIN_W - GLR_PAD:], (8, GLR_PAD))
        logits = _dot(glr, wg2_ref[...])[0:1] + bg_ref[...]
        eg = jnp.exp(_log_sigmoid(logits) * (1.0 / GLA_GATE_NORM))
        pf = p.astype(F32)
        q = pf[:, :GLA_QK_W] * (GLA_DK ** -0.5)
        k = pf[:, GLA_QK_W:2 * GLA_QK_W]
        v = pf[:, 2 * GLA_QK_W:2 * GLA_QK_W + GLA_V_W]
        gr = pf[:, 2 * GLA_QK_W + GLA_V_W:2 * GLA_QK_W + 2 * GLA_V_W]
        for h in range(GLA_HEADS):
            ks = slice(h * GLA_DK, (h + 1) * GLA_DK)
            vs = slice(h * GLA_DV, (h + 1) * GLA_DV)
            s_new = s0_ref[j, h] * _to_col(eg[:, ks]) + _to_col(k[:, ks]) * v[:, vs]
            s_ref[j, h] = s_new
            o = jnp.sum(_to_col(q[:, ks]) * s_new, axis=0, keepdims=True)
            o_ref[j, :, vs] = (_rms(o) * gn_ref[...] * _silu(gr[:, vs])).astype(BF16)


def _gla_sample(pm, s0, wg2p, bg, gn):
    n = pm.shape[0]
    seqs = GLA_SAMPLE_SEQS if n % GLA_SAMPLE_SEQS == 0 else 1
    st_spec = pl.BlockSpec((seqs, GLA_HEADS, GLA_DK, GLA_DV), lambda b: (b, 0, 0, 0))
    return pl.pallas_call(
        _gla_step_kernel,
        grid=(n // seqs,),
        in_specs=[pl.BlockSpec((seqs, 1, MAIN_W), lambda b: (b, 0, 0)), st_spec,
                  _resident(wg2p.shape), _resident(bg.shape), _resident(gn.shape)],
        out_specs=[pl.BlockSpec((seqs, 1, GLA_V_W), lambda b: (b, 0, 0)), st_spec],
        out_shape=[jax.ShapeDtypeStruct((n, 1, GLA_V_W), BF16),
                   jax.ShapeDtypeStruct(s0.shape, F32)],
        compiler_params=_params(1),
        name="gla_sample",
    )(pm, s0, wg2p, bg, gn)


ATT_MAX_STEP_BLOCKS = 4


def _att_kernel(q_ref, kp_ref, kc_ref, vp_ref, vc_ref, bias_ref, *rest):
    if len(rest) == 2:
        o_ref, lse_ref = rest
    else:
        o_ref, lse_ref = rest[5:7]
        step = (pl.program_id(0) * pl.num_programs(1) + pl.program_id(1)) * pl.num_programs(2) + pl.program_id(2)
        _att_step(step, *rest[:5], *rest[7:])
    kk = jnp.concatenate([kp_ref[...], kc_ref[...]], axis=0)
    vv = jnp.concatenate([vp_ref[...], vc_ref[...]], axis=0)
    lane = lax.broadcasted_iota(jnp.int32, (ATT_BLOCK, LANE), 1)
    low = lane < ATT_E
    scale = ATT_E ** -0.5
    half = (jnp.where(low, scale, 0.0).astype(BF16), jnp.where(low, 0.0, scale).astype(BF16))
    for blk in range(q_ref.shape[0] // ATT_BLOCK):
        first = jnp.where(pl.program_id(2) == 0, 1, 0) if blk == 0 else 0
        rows = slice(blk * ATT_BLOCK, (blk + 1) * ATT_BLOCK)
        keys = slice(blk * ATT_BLOCK, (blk + 2) * ATT_BLOCK)
        lse_tile = jnp.zeros((ATT_BLOCK, LANE), F32)
        for p in range(ATT_HPG // 2):
            cs = slice(p * LANE, (p + 1) * LANE)
            qp, kp, vp = q_ref[rows, cs], kk[keys, cs], vv[keys, cs]
            outs = []
            for hh in range(2):
                h = 2 * p + hh
                s = lax.dot_general(qp * half[hh], kp, _NT, preferred_element_type=F32)
                s = s + bias_ref[first, h]
                m = jnp.max(s, axis=-1, keepdims=True)
                e = jnp.exp(s - m)
                den = jnp.sum(e, axis=-1, keepdims=True)
                outs.append(_dot(e.astype(BF16), vp) / den)
                lse_tile = jnp.where(lane == h, m + jnp.log(den), lse_tile)
            o_ref[rows, cs] = jnp.where(low, outs[0], outs[1]).astype(BF16)
        lse_ref[rows, :] = lse_tile


def _att_steps(pa):
    bsz, dil, l, _ = pa.shape
    step_blocks = min(ATT_MAX_STEP_BLOCKS, l // ATT_BLOCK)
    return (bsz, dil, l // (step_blocks * ATT_BLOCK)), step_blocks


def _att_prompt(pa, bias2, g, side=None):
    bsz, dil, l, _ = pa.shape
    grid, step_blocks = _att_steps(pa)
    step_rows = step_blocks * ATT_BLOCK
    own = lambda part: pl.BlockSpec((None, None, step_rows, ATT_GW),
                                    functools.partial(lambda b, r, n, part: (b, r, n, part), part=part))
    prev = lambda part: pl.BlockSpec(
        (None, None, ATT_BLOCK, ATT_GW),
        functools.partial(lambda b, r, n, part: (b, r, jnp.maximum(step_blocks * n - 1, 0), part), part=part))
    out_spec = lambda width: pl.BlockSpec((None, None, step_rows, width), lambda b, r, n: (b, r, n, 0))
    in_specs = [own(0), prev(1), own(1), prev(2), own(2), _resident(bias2.shape)]
    out_specs = [out_spec(ATT_GW), out_spec(LANE)]
    out_shape = [jax.ShapeDtypeStruct((bsz, dil, l, ATT_GW), BF16),
                 jax.ShapeDtypeStruct((bsz, dil, l, LANE), F32)]
    args = [pa, pa, pa, pa, pa, bias2]
    if side is not None:
        new_t, cache_t, sb, sb0 = side[1:]
        assert cache_t.shape[0] == np.prod(grid), "one sample sequence per attention step"
        s_in, s_out, s_shape = _att_sample_specs(new_t, cache_t, sb, sb0, 1,
                                                 lambda b, r, n: (b * grid[1] + r) * grid[2] + n)
        in_specs += s_in
        out_specs += s_out
        out_shape += s_shape
        args += list(side)
    return pl.pallas_call(
        _att_kernel,
        grid=grid,
        in_specs=in_specs,
        out_specs=out_specs,
        out_shape=out_shape,
        compiler_params=_params(3),
        name=f"att_prompt_g{g}",
    )(*args)


def _att_step_kernel(q_ref, new_ref, buf_ref, sb_ref, sb0_ref, o_ref, l_ref, out_ref):
    _att_step(pl.program_id(0) * buf_ref.shape[0], q_ref, new_ref, buf_ref, sb_ref, sb0_ref, o_ref, l_ref, out_ref)


def _att_step(first_seq, q_ref, new_ref, buf_ref, sb_ref, sb0_ref, o_ref, l_ref, out_ref):
    seqs, _, _, _, lb = buf_ref.shape
    n = new_ref.shape[-1]
    scale = ATT_E ** -0.5
    head = lax.broadcasted_iota(jnp.int32, (ATT_HPG, 1), 0)
    last = lax.broadcasted_iota(jnp.int32, (1, lb), 1) == lb - 1
    for j in range(seqs):
        b = first_seq + j
        mine = lax.broadcasted_iota(jnp.int32, (1, n), 1) == b
        qb = q_ref[j].astype(BF16)
        s = jnp.zeros((ATT_HPG, lb), F32)
        sn = jnp.zeros((ATT_HPG, n), F32)
        for h in range(ATT_HPG):
            s = jnp.where(head == h, _dot(qb, buf_ref[j, 0, h].astype(BF16)), s)
            sn = jnp.where(head == h, _dot(qb, new_ref[0, h].astype(BF16)), sn)
        s = s * scale + sb_ref[...]
        sn = jnp.where(mine, sn * scale + sb0_ref[...], NEG_BIG)
        m = jnp.maximum(jnp.max(s, axis=-1, keepdims=True), jnp.max(sn, axis=-1, keepdims=True))
        e = jnp.exp(s - m)
        en = jnp.exp(sn - m)
        den = jnp.sum(e, axis=-1, keepdims=True) + jnp.sum(en, axis=-1, keepdims=True)
        pb = (e / den).astype(BF16)
        pnb = (en / den).astype(BF16)
        o = jnp.zeros((ATT_HPG, ATT_E), F32)
        for h in range(ATT_HPG):
            oh = (lax.dot_general(pb, buf_ref[j, 1, h].astype(BF16), _NT, preferred_element_type=F32)
                  + lax.dot_general(pnb, new_ref[1, h].astype(BF16), _NT, preferred_element_type=F32))
            o = jnp.where(head == h, oh, o)
        o_ref[j] = o
        l_ref[j] = jnp.broadcast_to(m + jnp.log(den), (ATT_HPG, LANE))
        for kv in range(2):
            for h in range(ATT_HPG):
                col = jnp.sum(jnp.where(mine, new_ref[kv, h], 0.0), axis=-1, keepdims=True)
                out_ref[j, kv, h] = jnp.where(last, col, pltpu.roll(buf_ref[j, kv, h], lb - 1, 1))


ATT_SAMPLE_STEP_BYTES = 8 * 1024 * 1024


def _att_sample(q, new_t, cache_t, sb, sb0, g):
    n, _, _, _, lb = cache_t.shape
    seqs = max(1, min(8, ATT_SAMPLE_STEP_BYTES // (2 * ATT_GW * lb * 4)))
    in_specs, out_specs, out_shape = _att_sample_specs(new_t, cache_t, sb, sb0, seqs, lambda i: i)
    return pl.pallas_call(
        _att_step_kernel,
        grid=(n // seqs,),
        in_specs=in_specs,
        out_specs=out_specs,
        out_shape=out_shape,
        compiler_params=_params(1),
        name=f"att_sample_g{g}",
    )(q, new_t, cache_t, sb, sb0)


def _att_sample_specs(new_t, cache_t, sb, sb0, seqs, block_of):
    n, _, _, _, lb = cache_t.shape
    planes = pl.BlockSpec((seqs, 2, ATT_HPG, ATT_E, lb), lambda *ids: (block_of(*ids), 0, 0, 0, 0))
    per_seq = lambda width: pl.BlockSpec((seqs, ATT_HPG, width), lambda *ids: (block_of(*ids), 0, 0))
    in_specs = [per_seq(ATT_E), _resident(new_t.shape), planes, _resident(sb.shape), _resident(sb0.shape)]
    out_specs = [per_seq(ATT_E), per_seq(LANE), planes]
    out_shape = [jax.ShapeDtypeStruct((n, ATT_HPG, ATT_E), F32),
                 jax.ShapeDtypeStruct((n, ATT_HPG, LANE), F32),
                 jax.ShapeDtypeStruct(cache_t.shape, F32)]
    return in_specs, out_specs, out_shape


def _mix_ffn_kernel(*refs, per_row_conv):
    (x_ref, oa_ref, o1_ref, o2_ref, o3_ref, l1_ref, l2_ref, l3_ref, ga_ref, gb_ref,
     g1_ref, sh2_ref, sc2_ref, g2_ref, n2_ref, nf_ref, et_ref,
     wa_ref, wb_ref, wo_ref, wup_ref, cw_ref, cb_ref, wdn_ref) = refs[:24]
    if per_row_conv:
        cs0_ref, cs1_ref, y_ref, cv_ref, so_ref, sl_ref = refs[24:]
    else:
        y_ref, cv_ref, so_ref, sl_ref, carry_ref = refs[24:]

        @pl.when(pl.program_id(1) == 0)
        def _():
            carry_ref[...] = jnp.zeros_like(carry_ref)
    tm = x_ref.shape[1]

    def token_order(ref, stage_ref):
        dil, _, width = ref.shape
        if dil == 1:
            return ref[0].astype(F32)
        for r in range(dil):
            plane = ref[r].astype(F32)
            for c in range(width // LANE):
                stage_ref[c, pl.ds(r, tm // dil, stride=dil), :] = plane[:, c * LANE:(c + 1) * LANE]
        return jnp.concatenate([stage_ref[c] for c in range(width // LANE)], axis=1)

    ls = [token_order(l_ref, sl_ref) for l_ref in (l1_ref, l2_ref, l3_ref)]
    m = jnp.maximum(jnp.maximum(ls[0], ls[1]), ls[2])
    ws = [jnp.exp(l - m) for l in ls]
    den = ws[0] + ws[1] + ws[2]
    o_att = jnp.zeros((tm, ATT_GW), F32)
    lane = lax.broadcasted_iota(jnp.int32, (tm, LANE), 1)
    for w, o_ref in zip(ws, (o1_ref, o2_ref, o3_ref)):
        wn = w / den
        hi = wn.astype(BF16).astype(F32)
        packed = jnp.where(lane < ATT_HPG, hi, pltpu.roll(wn - hi, ATT_HPG, 1)).astype(BF16)
        o_att = o_att + _dot(packed, et_ref[...]) * token_order(o_ref, so_ref)

    y_a = _dot(oa_ref[0], wa_ref[...])
    y_b = _dot(o_att.astype(BF16), wb_ref[...])
    mix_in = jax.nn.sigmoid(ga_ref[0].astype(F32)) * y_a + jax.nn.sigmoid(gb_ref[0].astype(F32)) * y_b
    x1 = x_ref[0] + g1_ref[0] * _dot(mix_in.astype(BF16), wo_ref[...])

    h2 = (_rms(x1) * n2_ref[...] * (1.0 + sc2_ref[0]) + sh2_ref[0]).astype(BF16)
    a = _dot(h2, wup_ref[:, :FFN_HIDDEN])
    bgate = _dot(h2, wup_ref[:, FFN_HIDDEN:])
    if per_row_conv:
        a2, a1 = cs0_ref[0], cs1_ref[0]
        cv_ref[0] = a
    else:
        row = lax.broadcasted_iota(jnp.int32, (tm, FFN_HIDDEN), 0)
        prev = carry_ref[...]
        a1 = jnp.where(row == 0, prev[7:8], pltpu.roll(a, 1, 0))
        a2 = jnp.where(row == 0, prev[6:7], jnp.where(row == 1, prev[7:8], pltpu.roll(a, 2, 0)))
        carry_ref[...] = a[tm - 8:, :]
        cv_ref[0] = a[tm - 8:, :]
    a_conv = cb_ref[...] + (cw_ref[0:1] * a2 + cw_ref[1:2] * a1 + cw_ref[2:3] * a)
    y_f = _dot((_silu(a_conv) * bgate).astype(BF16), wdn_ref[...])
    x2 = x1 + g2_ref[0] * y_f
    y_ref[0] = _rms(x2) * nf_ref[...]


def _mix_ffn(x, oa, os_, ls_, pm, ada, n2g, nfg, emat_t, wa, wb, wo, wup, cw, cb, wdn, tm, conv_state=None):
    bsz, t, _ = x.shape
    per_row = conv_state is not None
    rows = lambda width, k=0: pl.BlockSpec(
        (1, tm, width), functools.partial(lambda b, i, k: (b, i, k), k=k))
    planes = lambda a: pl.BlockSpec((None, a.shape[1], tm // a.shape[1], a.shape[3]),
                                    lambda b, i: (b, 0, i, 0))
    in_specs = ([rows(D_MODEL), rows(GLA_V_W)] + [planes(a) for a in (*os_, *ls_)]
                + [rows(D_MODEL, 3), rows(D_MODEL, 4)]
                + [_ada_spec(ada, tm, k) for k in (2, 3, 4, 5)]
                + [_resident(a.shape) for a in (n2g, nfg, emat_t, wa, wb, wo, wup, cw, cb, wdn)])
    args = [x, oa, *os_, *ls_, pm, pm, ada, ada, ada, ada, n2g, nfg, emat_t, wa, wb, wo, wup, cw, cb, wdn]
    scratch = [pltpu.VMEM((ATT_GW // LANE, tm, LANE), F32), pltpu.VMEM((1, tm, LANE), F32)]
    if per_row:
        in_specs += [rows(FFN_HIDDEN)] * 2
        args += [conv_state[:, 0][None], conv_state[:, 1][None]]
        cv_spec = rows(FFN_HIDDEN)
        cv_shape = jax.ShapeDtypeStruct((bsz, t, FFN_HIDDEN), F32)
    else:
        cv_spec = pl.BlockSpec((1, 8, FFN_HIDDEN), lambda b, i: (b, 0, 0))
        cv_shape = jax.ShapeDtypeStruct((bsz, 8, FFN_HIDDEN), F32)
        scratch.append(pltpu.VMEM((8, FFN_HIDDEN), F32))
    return pl.pallas_call(
        functools.partial(_mix_ffn_kernel, per_row_conv=per_row),
        grid=(bsz, t // tm),
        in_specs=in_specs,
        out_specs=[rows(D_MODEL), cv_spec],
        out_shape=[jax.ShapeDtypeStruct((bsz, t, D_MODEL), F32), cv_shape],
        scratch_shapes=scratch,
        compiler_params=_params(2),
        name="mix_ffn_sample" if per_row else "mix_ffn_prompt",
    )(*args)


def _t5_bucket(dist):
    max_exact = REL_BUCKETS // 2
    d = np.maximum(dist, 1).astype(np.float32)
    large = max_exact + (np.log(d / max_exact) / np.log(REL_MAX_DIST / max_exact)
                         * (REL_BUCKETS - max_exact)).astype(np.int32)
    large = np.minimum(large, REL_BUCKETS - 1)
    return np.where(dist < max_exact, dist, large).astype(np.int32)


def _prompt_bias(rel_bias_g, dil):
    i = np.arange(ATT_BLOCK)[:, None]
    j = np.arange(2 * ATT_BLOCK)[None, :]
    off = ATT_BLOCK + i - j
    valid = (off >= 0) & (off <= ATT_SPAN)
    by_off = rel_bias_g[_t5_bucket(np.arange(ATT_SPAN + 1) * dil)].T
    period = 4 * ATT_BLOCK
    seq = jnp.pad(by_off[:, ::-1], ((0, 0), (0, period - ATT_SPAN - 1)))
    rows = jnp.tile(seq, (1, ATT_BLOCK))[:, :ATT_BLOCK * (period - 1)]
    bias = rows.reshape(ATT_HPG, ATT_BLOCK, period - 1)[:, :, :2 * ATT_BLOCK]
    normal = jnp.where(valid[None], bias, NEG_BIG)
    first = jnp.where((valid & (j >= ATT_BLOCK))[None], bias, NEG_BIG)
    return jnp.stack([normal, first])


def _sample_bias(rel_bias_g, lb, dil):
    assert lb == ATT_SPAN * dil
    by_off = rel_bias_g[_t5_bucket(np.arange(ATT_SPAN + 1) * dil)].T
    strided = by_off[:, :0:-1, None]
    sb = jnp.pad(strided, ((0, 0), (0, 0), (0, dil - 1)), constant_values=NEG_BIG).reshape(ATT_HPG, lb)
    return sb, by_off[:, :1]


def _head_indicator_t():
    r = np.arange(LANE)[:, None]
    e = ((r < 2 * ATT_HPG) & (r % ATT_HPG == np.arange(ATT_GW)[None, :] // ATT_E)).astype(np.float32)
    return jnp.asarray(e, BF16)


def _permute_w_in(w_in):
    glr = jnp.pad(w_in[:, W_ATT0:W_ATT0 + GLA_GATE_RANK], ((0, 0), (0, GLR_PAD - GLA_GATE_RANK)))
    return jnp.concatenate([w_in[:, :W_ATT0], w_in[:, W_ATT0 + GLA_GATE_RANK:], glr], axis=1).astype(BF16)


PROMPT_TM = 256
SAMPLE_TM = 128
GLA_TC = 512


def kernel(x_prompt, x_sample, state_gla, cache_win1, cache_win2, cache_win3, state_conv, c_prompt, c_sample,
           w_ada, b_ada, norm1_g, w_in, gla_wg2, gla_bg, gla_norm_g, w_branch_a, w_branch_b, w_out, rel_bias,
           norm2_g, w_up, conv_w, conv_b, w_down, normf_g):
    bp, tp, _ = x_prompt.shape
    ns = x_sample.shape[0]
    assert w_ada.shape[0] == 1, "single-layer trunk"

    w_all = _permute_w_in(w_in[0])
    wg2p = jnp.pad(gla_wg2[0], ((0, GLR_PAD - GLA_GATE_RANK), (0, 0))).astype(BF16)
    bg = gla_bg[0][None]
    gn = gla_norm_g[0][None]
    n1g, n2g, nfg = norm1_g[0][None], norm2_g[0][None], normf_g[None]
    wa, wb, wo = w_branch_a[0].astype(BF16), w_branch_b[0].astype(BF16), w_out[0].astype(BF16)
    wup, wdn = w_up[0].astype(BF16), w_down[0].astype(BF16)
    cw, cb = conv_w[0], conv_b[0][None]
    emat_t = _head_indicator_t()
    groups = range(len(ATT_GROUPS))
    rel_g = [rel_bias[:, g * ATT_HPG:(g + 1) * ATT_HPG] for g in groups]

    ada = _ada(jnp.concatenate([c_prompt, c_sample], axis=0), w_ada[0].astype(BF16), b_ada[0][None])
    ada_p = ada[:bp][:, None, :]
    ada_s = ada[bp:][None]
    xs = x_sample.reshape(1, ns, D_MODEL)

    dils = tuple(d for _, d in ATT_GROUPS)
    pm_p, *rest = _inproj(x_prompt, ada_p, n1g, w_all, PROMPT_TM, tuple(min(w, tp) for w, _ in ATT_GROUPS), dils)
    pa_p, tails_p = rest[:3], rest[3:]
    pm_s, *rest = _inproj(xs, ada_s, n1g, w_all, SAMPLE_TM, (ns, ns, ns), (1, 1, 1))
    pa_s, new_t = rest[:3], rest[3:]

    sample_att = []
    for g, cache in enumerate((cache_win1, cache_win2, cache_win3)):
        cache_t = jnp.transpose(cache[0], (0, 2, 3, 4, 1))
        q = pa_s[g][0, 0, :, :ATT_GW].astype(F32).reshape(ns, ATT_HPG, ATT_E)
        sample_att.append((q, new_t[g][0], cache_t) + _sample_bias(rel_g[g], cache.shape[2], dils[g]))
    hosts = {"gla": bp * (tp // GLA_TC)}
    hosts.update({g: int(np.prod(_att_steps(pa_p[g])[0])) for g in groups})
    rider = {}
    for sg in reversed(groups):
        host = next((h for h, steps in hosts.items() if steps == ns and h not in rider), None)
        if host is not None:
            rider[host] = sg
    res_s = [None if g in rider.values() else _att_sample(*sample_att[g], g) for g in groups]
    side = lambda host: sample_att[rider[host]] if host in rider else None

    oa_p, gla_p, *rode = _gla_prompt(pm_p, wg2p, bg, gn, GLA_TC, side("gla"))
    if rode:
        res_s[rider["gla"]] = rode
    os_p, ls_p = [], []
    for g in groups:
        o, l, *rode = _att_prompt(pa_p[g], _prompt_bias(rel_g[g], dils[g]), g, side(g))
        if rode:
            res_s[rider[g]] = rode
        os_p.append(o)
        ls_p.append(l)
    y_p, cv_p = _mix_ffn(x_prompt, oa_p, os_p, ls_p, pm_p, ada_p, n2g, nfg, emat_t,
                         wa, wb, wo, wup, cw, cb, wdn, PROMPT_TM)

    oa_s, gla_s = _gla_sample(pm_s.reshape(ns, 1, MAIN_W), state_gla[0], wg2p, bg, gn)
    os_s, ls_s, new_caches = [], [], []
    for o, l, shifted in res_s:
        os_s.append(o.reshape(1, 1, ns, ATT_GW).astype(BF16))
        ls_s.append(jnp.pad(l[:, :, 0], ((0, 0), (0, LANE - ATT_HPG)))[None, None])
        new_caches.append(jnp.transpose(shifted, (0, 4, 1, 2, 3))[None])
    y_s, cv_s = _mix_ffn(xs, oa_s.reshape(1, ns, GLA_V_W), os_s, ls_s, pm_s, ada_s, n2g, nfg, emat_t,
                         wa, wb, wo, wup, cw, cb, wdn, SAMPLE_TM, conv_state=state_conv[0])

    tails = [jnp.transpose(c, (0, 4, 1, 2, 3))[None] for c in tails_p]
    conv_s = jnp.stack([state_conv[0][:, 1], cv_s[0]], axis=1)[None]
    return (y_p, y_s.reshape(ns, 1, D_MODEL), gla_p[None], gla_s[None],
            tails[0], new_caches[0], tails[1], new_caches[1], tails[2], new_caches[2],
            cv_p[:, 6:8][None], conv_s)
```

```python
import functools

import numpy as np
import jax
import jax.numpy as jnp
from jax import lax
from jax.experimental import pallas as pl
from jax.experimental.pallas import tpu as pltpu

F32 = jnp.float32
BF16 = jnp.bfloat16

D_MODEL = 1024
GLA_HEADS = 4
GLA_DK = 128
GLA_DV = 256
GLA_QK_W = GLA_HEADS * GLA_DK
GLA_V_W = GLA_HEADS * GLA_DV
GLA_GATE_RANK = 16
GLA_GATE_NORM = 16.0
GLA_CHUNK = 64
ATT_GROUPS = ((128, 1), (512, 4), (2048, 16))
ATT_HPG = 8
ATT_E = 64
ATT_GW = ATT_HPG * ATT_E
ATT_BLOCK = 128
ATT_SPAN = 128
REL_BUCKETS = 32
REL_MAX_DIST = 2048
FFN_HIDDEN = 2816
N_ADA = 6
EPS = 1e-6
NEG_BIG = -1e30

LANE = 128
GLR_PAD = LANE
MAIN_W = 2 * GLA_QK_W + 2 * GLA_V_W + 2 * D_MODEL + GLR_PAD
ATT_W3 = 3 * 3 * ATT_GW
W_ATT0 = 2 * GLA_QK_W + 2 * GLA_V_W
COL_CHUNK = 512
VMEM_LIMIT = 56 * 1024 * 1024

_NT = (((1,), (1,)), ((), ()))
_TN = (((0,), (0,)), ((), ()))


def _params(n_axes):
    return pltpu.CompilerParams(dimension_semantics=("arbitrary",) * n_axes,
                                vmem_limit_bytes=VMEM_LIMIT)


def _resident(shape):
    zeros = (0,) * len(shape)
    return pl.BlockSpec(shape, lambda *_: zeros, pipeline_mode=pl.Buffered(1))


def _silu(x):
    return x * jax.nn.sigmoid(x)


def _rms(x):
    return x * lax.rsqrt(jnp.mean(x * x, axis=-1, keepdims=True) + EPS)


def _split3(x):
    a = x.astype(BF16)
    r = x - a.astype(F32)
    b = r.astype(BF16)
    c = (r - b.astype(F32)).astype(BF16)
    return a, b, c


def _dot(a, b):
    return jnp.dot(a, b, preferred_element_type=F32)


def _ada_kernel(c_ref, w_ref, b_ref, o_ref):
    s = _silu(c_ref[...]).astype(BF16)
    o_ref[...] = _dot(s, w_ref[...]) + b_ref[...]


def _ada(c_all, w_ada, b_ada):
    m = c_all.shape[0]
    n = w_ada.shape[1]
    tn = D_MODEL
    return pl.pallas_call(
        _ada_kernel,
        grid=(n // tn,),
        in_specs=[pl.BlockSpec((m, D_MODEL), lambda j: (0, 0)),
                  pl.BlockSpec((D_MODEL, tn), lambda j: (0, j)),
                  pl.BlockSpec((1, tn), lambda j: (0, j))],
        out_specs=pl.BlockSpec((m, tn), lambda j: (0, j)),
        out_shape=jax.ShapeDtypeStruct((m, n), F32),
        compiler_params=_params(1),
        name="ada",
    )(c_all, w_ada, b_ada)


def _ada_spec(ada, tm, k):
    if ada.shape[1] == 1:
        return pl.BlockSpec((1, 1, D_MODEL), lambda b, i: (b, 0, k))
    return pl.BlockSpec((1, tm, D_MODEL), lambda b, i: (b, i, k))


def _inproj_kernel(x_ref, sh_ref, sc_ref, g_ref, w_ref, pm_ref, pa1_ref, pa2_ref, pa3_ref,
                   c1_ref, c2_ref, c3_ref, stage_ref):
    tm = x_ref.shape[1]
    h = _rms(x_ref[0]) * g_ref[...]
    h = h * (1.0 + sc_ref[0]) + sh_ref[0]
    hb = h.astype(BF16)
    for c0 in range(0, MAIN_W, COL_CHUNK):
        c1 = min(c0 + COL_CHUNK, MAIN_W)
        w0 = c0 if c0 < W_ATT0 else c0 + ATT_W3
        pm_ref[0, :, c0:c1] = _dot(hb, w_ref[:, w0:w0 + c1 - c0]).astype(BF16)
    for g, (pa_ref, cache_ref) in enumerate(zip((pa1_ref, pa2_ref, pa3_ref), (c1_ref, c2_ref, c3_ref))):
        dil = pa_ref.shape[0]
        rows = cache_ref.shape[-1]
        for part in range(3):
            cs = slice(part * ATT_GW, (part + 1) * ATT_GW)
            c0 = W_ATT0 + part * 3 * ATT_GW + g * ATT_GW
            acc = _dot(hb, w_ref[:, c0:c0 + ATT_GW])
            if dil == 1:
                pa_ref[0, :, cs] = acc.astype(BF16)
            else:
                stage = stage_ref.at[3 * g + part]
                for c in range(ATT_GW // LANE):
                    stage[c] = acc[:, c * LANE:(c + 1) * LANE]
                for r in range(dil):
                    for c in range(ATT_GW // LANE):
                        c0 = part * ATT_GW + c * LANE
                        pa_ref[r, :, c0:c0 + LANE] = stage[c, pl.ds(r, tm // dil, stride=dil), :].astype(BF16)
            if part:
                tail = acc[tm - rows:, :].T
                cache_ref[part - 1] = tail.reshape(ATT_HPG, ATT_E, rows)


def _inproj(x, ada, norm_g, w_all, tm, windows, dils):
    bsz, t, _ = x.shape
    nt = t // tm
    pa_specs, pa_shapes, cache_specs, cache_shapes = [], [], [], []
    for w, dil in zip(windows, dils):
        rows = min(tm, w)
        nblk = w // rows
        pa_specs.append(pl.BlockSpec((None, dil, tm // dil, 3 * ATT_GW), lambda b, i: (b, 0, i, 0)))
        pa_shapes.append(jax.ShapeDtypeStruct((bsz, dil, t // dil, 3 * ATT_GW), BF16))
        cache_specs.append(pl.BlockSpec(
            (None, 2, ATT_HPG, ATT_E, rows),
            functools.partial(lambda b, i, first: (b, 0, 0, 0, jnp.maximum(i - first, 0)), first=nt - nblk)))
        cache_shapes.append(jax.ShapeDtypeStruct((bsz, 2, ATT_HPG, ATT_E, w), F32))
    return pl.pallas_call(
        _inproj_kernel,
        grid=(bsz, nt),
        in_specs=[pl.BlockSpec((1, tm, D_MODEL), lambda b, i: (b, i, 0)),
                  _ada_spec(ada, tm, 0), _ada_spec(ada, tm, 1),
                  _resident((1, D_MODEL)), _resident(w_all.shape)],
        out_specs=[pl.BlockSpec((1, tm, MAIN_W), lambda b, i: (b, i, 0))] + pa_specs + cache_specs,
        out_shape=[jax.ShapeDtypeStruct((bsz, t, MAIN_W), BF16)] + pa_shapes + cache_shapes,
        scratch_shapes=[pltpu.VMEM((3 * len(windows), ATT_GW // LANE, tm, LANE), F32)],
        compiler_params=_params(2),
        name="inproj",
    )(x, ada, ada, norm_g, w_all)


def _log_sigmoid(x):
    return jnp.minimum(x, 0.0) - jnp.log1p(jnp.exp(-jnp.abs(x)))


def _gla_kernel(qk_ref, v_ref, gr_ref, glr_ref, wg2_ref, bg_ref, gn_ref, tri_ref, *rest):
    scratch = rest[-7:]
    rest = rest[:-7]
    if len(rest) == 2:
        side_in, (o_ref, s_ref), side_out = (), rest, ()
    else:
        side_in, (o_ref, s_ref), side_out = rest[:5], rest[5:7], rest[7:10]
    st_ref, qi_ref, ki_ref, qb_ref, kd_ref, oraw_ref, u_ref = scratch
    t = pl.program_id(1)
    c = GLA_CHUNK
    tc = qk_ref.shape[1]
    n_chunks = tc // c

    @pl.when(t == 0)
    def _():
        st_ref[...] = jnp.zeros_like(st_ref)

    if side_in:
        _att_step(pl.program_id(0) * pl.num_programs(1) + t, *side_in, *side_out)

    logits = _dot(glr_ref[0], wg2_ref[...]) + bg_ref[...]
    g = _log_sigmoid(logits) * (1.0 / GLA_GATE_NORM)
    b = _dot_exact_lhs_t(tri_ref[...], g).reshape(n_chunks, c, GLA_QK_W)
    bref = b[:, c // 2:c // 2 + 1, :]
    blast = b[:, c - 1:c, :]
    qk = qk_ref[0].astype(F32).reshape(n_chunks, c, 2 * GLA_QK_W)
    q = qk[:, :, :GLA_QK_W] * (GLA_DK ** -0.5)
    k = qk[:, :, GLA_QK_W:]
    qi_ref[...] = (q * jnp.exp(b - bref)).astype(BF16)
    ki_ref[...] = (k * jnp.exp(bref - b)).astype(BF16)
    qb_ref[...] = (q * jnp.exp(b)).astype(BF16)
    kd_ref[...] = (k * jnp.exp(blast - b)).astype(BF16)
    e_last = jnp.exp(blast)

    row = lax.broadcasted_iota(jnp.int32, (c, c), 0)
    col = lax.broadcasted_iota(jnp.int32, (c, c), 1)
    tril = row >= col
    for ci in range(n_chunks):
        rows = slice(ci * c, (ci + 1) * c)
        for h in range(GLA_HEADS):
            ks = slice(h * GLA_DK, (h + 1) * GLA_DK)
            vs = slice(h * GLA_DV, (h + 1) * GLA_DV)
            vh = v_ref[0, rows, vs]
            a = lax.dot_general(qi_ref[ci, :, ks], ki_ref[ci, :, ks], _NT, preferred_element_type=F32)
            a = jnp.where(tril, a, 0.0)
            oraw_ref[rows, vs] = _dot(a.astype(BF16), vh)
            u_ref[ci, h] = lax.dot_general(vh, kd_ref[ci, :, ks], _TN, preferred_element_type=F32)
    for ci in range(n_chunks):
        rows = slice(ci * c, (ci + 1) * c)
        for h in range(GLA_HEADS):
            ks = slice(h * GLA_DK, (h + 1) * GLA_DK)
            vs = slice(h * GLA_DV, (h + 1) * GLA_DV)
            st = st_ref[h]
            oraw_ref[rows, vs] += lax.dot_general(
                qb_ref[ci, :, ks], st.astype(BF16), _NT, preferred_element_type=F32)
            st_ref[h] = st * e_last[ci, :, ks] + u_ref[ci, h]

    for h in range(GLA_HEADS):
        vs = slice(h * GLA_DV, (h + 1) * GLA_DV)
        o_ref[0, :, vs] = (_rms(oraw_ref[:, vs]) * gn_ref[...]
                           * _silu(gr_ref[0, :, vs].astype(F32))).astype(BF16)

    @pl.when(t == pl.num_programs(1) - 1)
    def _():
        for h in range(GLA_HEADS):
            s_ref[0, h] = st_ref[h].T


def _dot_exact_lhs_t(m, x):
    a, b, c = _split3(x)
    return _dot(m, a) + _dot(m, b) + _dot(m, c)


def _gla_prompt(pm, wg2p, bg, gn, tc, side=None):
    bsz, t, _ = pm.shape
    nt = t // tc
    blk = lambda k: pl.BlockSpec((1, tc, D_MODEL), functools.partial(lambda b, i, k: (b, i, k), k=k))
    pos = np.arange(tc)
    same_chunk = pos[:, None] // GLA_CHUNK == pos[None, :] // GLA_CHUNK
    tri = jnp.asarray(same_chunk & (pos[:, None] >= pos[None, :]), BF16)
    in_specs = [blk(0), blk(1), blk(2),
                pl.BlockSpec((1, tc, GLR_PAD), lambda b, i: (b, i, (MAIN_W - GLR_PAD) // GLR_PAD)),
                _resident(wg2p.shape), _resident(bg.shape), _resident(gn.shape), _resident(tri.shape)]
    out_specs = [pl.BlockSpec((1, tc, GLA_V_W), lambda b, i: (b, i, 0)),
                 pl.BlockSpec((1, GLA_HEADS, GLA_DK, GLA_DV), lambda b, i: (b, 0, 0, 0))]
    out_shape = [jax.ShapeDtypeStruct((bsz, t, GLA_V_W), BF16),
                 jax.ShapeDtypeStruct((bsz, GLA_HEADS, GLA_DK, GLA_DV), F32)]
    args = [pm, pm, pm, pm, wg2p, bg, gn, tri]
    staged = pltpu.VMEM((tc // GLA_CHUNK, GLA_CHUNK, GLA_QK_W), BF16)
    if side is not None:
        q, new_t, cache_t, sb, sb0 = side
        assert cache_t.shape[0] == bsz * nt, "one sample sequence per scan step"
        s_in, s_out, s_shape = _att_sample_specs(new_t, cache_t, sb, sb0, 1, lambda b, i: b * nt + i)
        in_specs += s_in
        out_specs += s_out
        out_shape += s_shape
        args += list(side)
    return pl.pallas_call(
        _gla_kernel,
        grid=(bsz, nt),
        in_specs=in_specs,
        out_specs=out_specs,
        out_shape=out_shape,
        scratch_shapes=[pltpu.VMEM((GLA_HEADS, GLA_DV, GLA_DK), F32), staged, staged, staged, staged,
                        pltpu.VMEM((tc, GLA_V_W), F32),
                        pltpu.VMEM((tc // GLA_CHUNK, GLA_HEADS, GLA_DV, GLA_DK), F32)],
        compiler_params=_params(2),
        name="gla_prompt",
    )(*args)


def _to_col(row):
    n = row.shape[1]
    eye = lax.broadcasted_iota(jnp.int32, (n, n), 0) == lax.broadcasted_iota(jnp.int32, (n, n), 1)
    return jnp.sum(jnp.where(eye, jnp.broadcast_to(row, (n, n)), 0.0), axis=1, keepdims=True)


GLA_SAMPLE_SEQS = 8


def _gla_step_kernel(pm_ref, s0_ref, wg2_ref, bg_ref, gn_ref, o_ref, s_ref):
    for j in range(pm_ref.shape[0]):
        p = pm_ref[j]
        glr = jnp.broadcast_to(p[:, MAIN_W - GLR_PAD:], (8, GLR_PAD))
        logits = _dot(glr, wg2_ref[...])[0:1] + bg_ref[...]
        eg = jnp.exp(_log_sigmoid(logits) * (1.0 / GLA_GATE_NORM))
        pf = p.astype(F32)
        q = pf[:, :GLA_QK_W] * (GLA_DK ** -0.5)
        k = pf[:, GLA_QK_W:2 * GLA_QK_W]
        v = pf[:, 2 * GLA_QK_W:2 * GLA_QK_W + GLA_V_W]
        gr = pf[:, 2 * GLA_QK_W + GLA_V_W:2 * GLA_QK_W + 2 * GLA_V_W]
        for h in range(GLA_HEADS):
            ks = slice(h * GLA_DK, (h + 1) * GLA_DK)
            vs = slice(h * GLA_DV, (h + 1) * GLA_DV)
            s_new = s0_ref[j, h] * _to_col(eg[:, ks]) + _to_col(k[:, ks]) * v[:, vs]
            s_ref[j, h] = s_new
            o = jnp.sum(_to_col(q[:, ks]) * s_new, axis=0, keepdims=True)
            o_ref[j, :, vs] = (_rms(o) * gn_ref[...] * _silu(gr[:, vs])).astype(BF16)


def _gla_sample(pm, s0, wg2p, bg, gn):
    n = pm.shape[0]
    seqs = GLA_SAMPLE_SEQS if n % GLA_SAMPLE_SEQS == 0 else 1
    st_spec = pl.BlockSpec((seqs, GLA_HEADS, GLA_DK, GLA_DV), lambda b: (b, 0, 0, 0))
    return pl.pallas_call(
        _gla_step_kernel,
        grid=(n // seqs,),
        in_specs=[pl.BlockSpec((seqs, 1, MAIN_W), lambda b: (b, 0, 0)), st_spec,
                  _resident(wg2p.shape), _resident(bg.shape), _resident(gn.shape)],
        out_specs=[pl.BlockSpec((seqs, 1, GLA_V_W), lambda b: (b, 0, 0)), st_spec],
        out_shape=[jax.ShapeDtypeStruct((n, 1, GLA_V_W), BF16),
                   jax.ShapeDtypeStruct(s0.shape, F32)],
        compiler_params=_params(1),
        name="gla_sample",
    )(pm, s0, wg2p, bg, gn)


ATT_MAX_STEP_BLOCKS = 4


def _att_kernel(q_ref, kp_ref, kc_ref, vp_ref, vc_ref, bias_ref, *rest):
    if len(rest) == 2:
        o_ref, lse_ref = rest
    else:
        o_ref, lse_ref = rest[5:7]
        step = (pl.program_id(0) * pl.num_programs(1) + pl.program_id(1)) * pl.num_programs(2) + pl.program_id(2)
        _att_step(step, *rest[:5], *rest[7:])
    kk = jnp.concatenate([kp_ref[...], kc_ref[...]], axis=0)
    vv = jnp.concatenate([vp_ref[...], vc_ref[...]], axis=0)
    lane = lax.broadcasted_iota(jnp.int32, (ATT_BLOCK, LANE), 1)
    low = lane < ATT_E
    scale = ATT_E ** -0.5
    half = (jnp.where(low, scale, 0.0).astype(BF16), jnp.where(low, 0.0, scale).astype(BF16))
    for blk in range(q_ref.shape[0] // ATT_BLOCK):
        first = jnp.where(pl.program_id(2) == 0, 1, 0) if blk == 0 else 0
        rows = slice(blk * ATT_BLOCK, (blk + 1) * ATT_BLOCK)
        keys = slice(blk * ATT_BLOCK, (blk + 2) * ATT_BLOCK)
        m_tile = jnp.zeros((ATT_BLOCK, LANE), F32)
        den_tile = jnp.ones((ATT_BLOCK, LANE), F32)
        for p in range(ATT_HPG // 2):
            cs = slice(p * LANE, (p + 1) * LANE)
            qp, kp, vp = q_ref[rows, cs], kk[keys, cs], vv[keys, cs]
            outs = []
            for hh in range(2):
                h = 2 * p + hh
                s = lax.dot_general(qp * half[hh], kp, _NT, preferred_element_type=F32)
                s = s + bias_ref[first, h]
                m = jnp.max(s, axis=-1, keepdims=True)
                e = jnp.exp(s - m)
                den = jnp.sum(e, axis=-1, keepdims=True)
                outs.append(_dot(e.astype(BF16), vp) / den)
                m_tile = jnp.where(lane == h, m, m_tile)
                den_tile = jnp.where(lane == h, den, den_tile)
            o_ref[rows, cs] = jnp.where(low, outs[0], outs[1]).astype(BF16)
        lse_ref[rows, :] = m_tile + jnp.log(den_tile)


def _att_steps(pa):
    bsz, dil, l, _ = pa.shape
    step_blocks = min(ATT_MAX_STEP_BLOCKS, l // ATT_BLOCK)
    return (bsz, dil, l // (step_blocks * ATT_BLOCK)), step_blocks


def _att_prompt(pa, bias2, g, side=None):
    bsz, dil, l, _ = pa.shape
    grid, step_blocks = _att_steps(pa)
    step_rows = step_blocks * ATT_BLOCK
    own = lambda part: pl.BlockSpec((None, None, step_rows, ATT_GW),
                                    functools.partial(lambda b, r, n, part: (b, r, n, part), part=part))
    prev = lambda part: pl.BlockSpec(
        (None, None, ATT_BLOCK, ATT_GW),
        functools.partial(lambda b, r, n, part: (b, r, jnp.maximum(step_blocks * n - 1, 0), part), part=part))
    out_spec = lambda width: pl.BlockSpec((None, None, step_rows, width), lambda b, r, n: (b, r, n, 0))
    in_specs = [own(0), prev(1), own(1), prev(2), own(2), _resident(bias2.shape)]
    out_specs = [out_spec(ATT_GW), out_spec(LANE)]
    out_shape = [jax.ShapeDtypeStruct((bsz, dil, l, ATT_GW), BF16),
                 jax.ShapeDtypeStruct((bsz, dil, l, LANE), F32)]
    args = [pa, pa, pa, pa, pa, bias2]
    if side is not None:
        new_t, cache_t, sb, sb0 = side[1:]
        assert cache_t.shape[0] == np.prod(grid), "one sample sequence per attention step"
        s_in, s_out, s_shape = _att_sample_specs(new_t, cache_t, sb, sb0, 1,
                                                 lambda b, r, n: (b * grid[1] + r) * grid[2] + n)
        in_specs += s_in
        out_specs += s_out
        out_shape += s_shape
        args += list(side)
    return pl.pallas_call(
        _att_kernel,
        grid=grid,
        in_specs=in_specs,
        out_specs=out_specs,
        out_shape=out_shape,
        compiler_params=_params(3),
        name=f"att_prompt_g{g}",
    )(*args)


def _att_step_kernel(q_ref, new_ref, buf_ref, sb_ref, sb0_ref, o_ref, l_ref, out_ref):
    _att_step(pl.program_id(0) * buf_ref.shape[0], q_ref, new_ref, buf_ref, sb_ref, sb0_ref, o_ref, l_ref, out_ref)


def _att_step(first_seq, q_ref, new_ref, buf_ref, sb_ref, sb0_ref, o_ref, l_ref, out_ref):
    seqs, _, _, _, lb = buf_ref.shape
    n = new_ref.shape[-1]
    scale = ATT_E ** -0.5
    head = lax.broadcasted_iota(jnp.int32, (ATT_HPG, 1), 0)
    last = lax.broadcasted_iota(jnp.int32, (1, lb), 1) == lb - 1
    for j in range(seqs):
        b = first_seq + j
        mine = lax.broadcasted_iota(jnp.int32, (1, n), 1) == b
        qb = q_ref[j].astype(BF16)
        s = jnp.zeros((ATT_HPG, lb), F32)
        sn = jnp.zeros((ATT_HPG, n), F32)
        for h in range(ATT_HPG):
            s = jnp.where(head == h, _dot(qb, buf_ref[j, 0, h].astype(BF16)), s)
            sn = jnp.where(head == h, _dot(qb, new_ref[0, h].astype(BF16)), sn)
        s = s * scale + sb_ref[...]
        sn = jnp.where(mine, sn * scale + sb0_ref[...], NEG_BIG)
        m = jnp.maximum(jnp.max(s, axis=-1, keepdims=True), jnp.max(sn, axis=-1, keepdims=True))
        e = jnp.exp(s - m)
        en = jnp.exp(sn - m)
        den = jnp.sum(e, axis=-1, keepdims=True) + jnp.sum(en, axis=-1, keepdims=True)
        pb = (e / den).astype(BF16)
        pnb = (en / den).astype(BF16)
        o = jnp.zeros((ATT_HPG, ATT_E), F32)
        for h in range(ATT_HPG):
            oh = (lax.dot_general(pb, buf_ref[j, 1, h].astype(BF16), _NT, preferred_element_type=F32)
                  + lax.dot_general(pnb, new_ref[1, h].astype(BF16), _NT, preferred_element_type=F32))
            o = jnp.where(head == h, oh, o)
        o_ref[j] = o
        l_ref[j] = jnp.broadcast_to(m + jnp.log(den), (ATT_HPG, LANE))
        for kv in range(2):
            for h in range(ATT_HPG):
                col = jnp.sum(jnp.where(mine, new_ref[kv, h], 0.0), axis=-1, keepdims=True)
                out_ref[j, kv, h] = jnp.where(last, col, pltpu.roll(buf_ref[j, kv, h], lb - 1, 1))


ATT_SAMPLE_STEP_BYTES = 8 * 1024 * 1024


def _att_sample(q, new_t, cache_t, sb, sb0, g):
    n, _, _, _, lb = cache_t.shape
    seqs = max(1, min(8, ATT_SAMPLE_STEP_BYTES // (2 * ATT_GW * lb * 4)))
    in_specs, out_specs, out_shape = _att_sample_specs(new_t, cache_t, sb, sb0, seqs, lambda i: i)
    return pl.pallas_call(
        _att_step_kernel,
        grid=(n // seqs,),
        in_specs=in_specs,
        out_specs=out_specs,
        out_shape=out_shape,
        compiler_params=_params(1),
        name=f"att_sample_g{g}",
    )(q, new_t, cache_t, sb, sb0)


def _att_sample_specs(new_t, cache_t, sb, sb0, seqs, block_of):
    n, _, _, _, lb = cache_t.shape
    planes = pl.BlockSpec((seqs, 2, ATT_HPG, ATT_E, lb), lambda *ids: (block_of(*ids), 0, 0, 0, 0))
    per_seq = lambda width: pl.BlockSpec((seqs, ATT_HPG, width), lambda *ids: (block_of(*ids), 0, 0))
    in_specs = [per_seq(ATT_E), _resident(new_t.shape), planes, _resident(sb.shape), _resident(sb0.shape)]
    out_specs = [per_seq(ATT_E), per_seq(LANE), planes]
    out_shape = [jax.ShapeDtypeStruct((n, ATT_HPG, ATT_E), F32),
                 jax.ShapeDtypeStruct((n, ATT_HPG, LANE), F32),
                 jax.ShapeDtypeStruct(cache_t.shape, F32)]
    return in_specs, out_specs, out_shape


def _mix_ffn_kernel(*refs, per_row_conv):
    (x_ref, oa_ref, o1_ref, o2_ref, o3_ref, l1_ref, l2_ref, l3_ref, ga_ref, gb_ref,
     g1_ref, sh2_ref, sc2_ref, g2_ref, n2_ref, nf_ref, et_ref,
     wa_ref, wb_ref, wo_ref, wup_ref, cw_ref, cb_ref, wdn_ref) = refs[:24]
    if per_row_conv:
        cs0_ref, cs1_ref, y_ref, cv_ref, so_ref, sl_ref = refs[24:]
    else:
        y_ref, cv_ref, so_ref, sl_ref, carry_ref = refs[24:]

        @pl.when(pl.program_id(1) == 0)
        def _():
            carry_ref[...] = jnp.zeros_like(carry_ref)
    tm = x_ref.shape[1]

    def token_order(ref, stage_ref):
        dil, _, width = ref.shape
        if dil == 1:
            return ref[0].astype(F32)
        for r in range(dil):
            plane = ref[r].astype(F32)
            for c in range(width // LANE):
                stage_ref[c, pl.ds(r, tm // dil, stride=dil), :] = plane[:, c * LANE:(c + 1) * LANE]
        return jnp.concatenate([stage_ref[c] for c in range(width // LANE)], axis=1)

    ls = [token_order(l_ref, sl_ref) for l_ref in (l1_ref, l2_ref, l3_ref)]
    m = jnp.maximum(jnp.maximum(ls[0], ls[1]), ls[2])
    ws = [jnp.exp(l - m) for l in ls]
    den = ws[0] + ws[1] + ws[2]
    o_att = jnp.zeros((tm, ATT_GW), F32)
    lane = lax.broadcasted_iota(jnp.int32, (tm, LANE), 1)
    for w, o_ref in zip(ws, (o1_ref, o2_ref, o3_ref)):
        wn = w / den
        hi = wn.astype(BF16).astype(F32)
        packed = jnp.where(lane < ATT_HPG, hi, pltpu.roll(wn - hi, ATT_HPG, 1)).astype(BF16)
        o_att = o_att + _dot(packed, et_ref[...]) * token_order(o_ref, so_ref)

    y_a = _dot(oa_ref[0], wa_ref[...])
    y_b = _dot(o_att.astype(BF16), wb_ref[...])
    mix_in = jax.nn.sigmoid(ga_ref[0].astype(F32)) * y_a + jax.nn.sigmoid(gb_ref[0].astype(F32)) * y_b
    x1 = x_ref[0] + g1_ref[0] * _dot(mix_in.astype(BF16), wo_ref[...])

    h2 = (_rms(x1) * n2_ref[...] * (1.0 + sc2_ref[0]) + sh2_ref[0]).astype(BF16)
    a = _dot(h2, wup_ref[:, :FFN_HIDDEN])
    bgate = _dot(h2, wup_ref[:, FFN_HIDDEN:])
    if per_row_conv:
        a2, a1 = cs0_ref[0], cs1_ref[0]
        cv_ref[0] = a
    else:
        row = lax.broadcasted_iota(jnp.int32, (tm, FFN_HIDDEN), 0)
        prev = carry_ref[...]
        a1 = jnp.where(row == 0, prev[7:8], pltpu.roll(a, 1, 0))
        a2 = jnp.where(row == 0, prev[6:7], jnp.where(row == 1, prev[7:8], pltpu.roll(a, 2, 0)))
        carry_ref[...] = a[tm - 8:, :]
        cv_ref[0] = a[tm - 8:, :]
    a_conv = cb_ref[...] + (cw_ref[0:1] * a2 + cw_ref[1:2] * a1 + cw_ref[2:3] * a)
    y_f = _dot((_silu(a_conv) * bgate).astype(BF16), wdn_ref[...])
    x2 = x1 + g2_ref[0] * y_f
    y_ref[0] = _rms(x2) * nf_ref[...]


def _mix_ffn(x, oa, os_, ls_, pm, ada, n2g, nfg, emat_t, wa, wb, wo, wup, cw, cb, wdn, tm, conv_state=None):
    bsz, t, _ = x.shape
    per_row = conv_state is not None
    rows = lambda width, k=0: pl.BlockSpec(
        (1, tm, width), functools.partial(lambda b, i, k: (b, i, k), k=k))
    planes = lambda a: pl.BlockSpec((None, a.shape[1], tm // a.shape[1], a.shape[3]),
                                    lambda b, i: (b, 0, i, 0))
    in_specs = ([rows(D_MODEL), rows(GLA_V_W)] + [planes(a) for a in (*os_, *ls_)]
                + [rows(D_MODEL, 3), rows(D_MODEL, 4)]
                + [_ada_spec(ada, tm, k) for k in (2, 3, 4, 5)]
                + [_resident(a.shape) for a in (n2g, nfg, emat_t, wa, wb, wo, wup, cw, cb, wdn)])
    args = [x, oa, *os_, *ls_, pm, pm, ada, ada, ada, ada, n2g, nfg, emat_t, wa, wb, wo, wup, cw, cb, wdn]
    scratch = [pltpu.VMEM((ATT_GW // LANE, tm, LANE), F32), pltpu.VMEM((1, tm, LANE), F32)]
    if per_row:
        in_specs += [rows(FFN_HIDDEN)] * 2
        args += [conv_state[:, 0][None], conv_state[:, 1][None]]
        cv_spec = rows(FFN_HIDDEN)
        cv_shape = jax.ShapeDtypeStruct((bsz, t, FFN_HIDDEN), F32)
    else:
        cv_spec = pl.BlockSpec((1, 8, FFN_HIDDEN), lambda b, i: (b, 0, 0))
        cv_shape = jax.ShapeDtypeStruct((bsz, 8, FFN_HIDDEN), F32)
        scratch.append(pltpu.VMEM((8, FFN_HIDDEN), F32))
    return pl.pallas_call(
        functools.partial(_mix_ffn_kernel, per_row_conv=per_row),
        grid=(bsz, t // tm),
        in_specs=in_specs,
        out_specs=[rows(D_MODEL), cv_spec],
        out_shape=[jax.ShapeDtypeStruct((bsz, t, D_MODEL), F32), cv_shape],
        scratch_shapes=scratch,
        compiler_params=_params(2),
        name="mix_ffn_sample" if per_row else "mix_ffn_prompt",
    )(*args)


def _t5_bucket(dist):
    max_exact = REL_BUCKETS // 2
    d = np.maximum(dist, 1).astype(np.float32)
    large = max_exact + (np.log(d / max_exact) / np.log(REL_MAX_DIST / max_exact)
                         * (REL_BUCKETS - max_exact)).astype(np.int32)
    large = np.minimum(large, REL_BUCKETS - 1)
    return np.where(dist < max_exact, dist, large).astype(np.int32)


def _prompt_bias(rel_bias_g, dil):
    i = np.arange(ATT_BLOCK)[:, None]
    j = np.arange(2 * ATT_BLOCK)[None, :]
    off = ATT_BLOCK + i - j
    valid = (off >= 0) & (off <= ATT_SPAN)
    by_off = rel_bias_g[_t5_bucket(np.arange(ATT_SPAN + 1) * dil)].T
    period = 4 * ATT_BLOCK
    seq = jnp.pad(by_off[:, ::-1], ((0, 0), (0, period - ATT_SPAN - 1)))
    rows = jnp.tile(seq, (1, ATT_BLOCK))[:, :ATT_BLOCK * (period - 1)]
    bias = rows.reshape(ATT_HPG, ATT_BLOCK, period - 1)[:, :, :2 * ATT_BLOCK]
    normal = jnp.where(valid[None], bias, NEG_BIG)
    first = jnp.where((valid & (j >= ATT_BLOCK))[None], bias, NEG_BIG)
    return jnp.stack([normal, first])


def _sample_bias(rel_bias_g, lb, dil):
    assert lb == ATT_SPAN * dil
    by_off = rel_bias_g[_t5_bucket(np.arange(ATT_SPAN + 1) * dil)].T
    strided = by_off[:, :0:-1, None]
    sb = jnp.pad(strided, ((0, 0), (0, 0), (0, dil - 1)), constant_values=NEG_BIG).reshape(ATT_HPG, lb)
    return sb, by_off[:, :1]


def _head_indicator_t():
    r = np.arange(LANE)[:, None]
    e = ((r < 2 * ATT_HPG) & (r % ATT_HPG == np.arange(ATT_GW)[None, :] // ATT_E)).astype(np.float32)
    return jnp.asarray(e, BF16)


def _permute_w_in(w_in):
    glr = jnp.pad(w_in[:, W_ATT0:W_ATT0 + GLA_GATE_RANK], ((0, 0), (0, GLR_PAD - GLA_GATE_RANK)))
    return jnp.concatenate([w_in[:, :W_ATT0], w_in[:, W_ATT0 + GLA_GATE_RANK:], glr], axis=1).astype(BF16)


PROMPT_TM = 256
SAMPLE_TM = 128
GLA_TC = 512


def kernel(x_prompt, x_sample, state_gla, cache_win1, cache_win2, cache_win3, state_conv, c_prompt, c_sample,
           w_ada, b_ada, norm1_g, w_in, gla_wg2, gla_bg, gla_norm_g, w_branch_a, w_branch_b, w_out, rel_bias,
           norm2_g, w_up, conv_w, conv_b, w_down, normf_g):
    bp, tp, _ = x_prompt.shape
    ns = x_sample.shape[0]
    assert w_ada.shape[0] == 1, "single-layer trunk"

    w_all = _permute_w_in(w_in[0])
    wg2p = jnp.pad(gla_wg2[0], ((0, GLR_PAD - GLA_GATE_RANK), (0, 0))).astype(BF16)
    bg = gla_bg[0][None]
    gn = gla_norm_g[0][None]
    n1g, n2g, nfg = norm1_g[0][None], norm2_g[0][None], normf_g[None]
    wa, wb, wo = w_branch_a[0].astype(BF16), w_branch_b[0].astype(BF16), w_out[0].astype(BF16)
    wup, wdn = w_up[0].astype(BF16), w_down[0].astype(BF16)
    cw, cb = conv_w[0], conv_b[0][None]
    emat_t = _head_indicator_t()
    groups = range(len(ATT_GROUPS))
    rel_g = [rel_bias[:, g * ATT_HPG:(g + 1) * ATT_HPG] for g in groups]

    ada = _ada(jnp.concatenate([c_prompt, c_sample], axis=0), w_ada[0].astype(BF16), b_ada[0][None])
    ada_p = ada[:bp][:, None, :]
    ada_s = ada[bp:][None]
    xs = x_sample.reshape(1, ns, D_MODEL)

    dils = tuple(d for _, d in ATT_GROUPS)
    pm_p, *rest = _inproj(x_prompt, ada_p, n1g, w_all, PROMPT_TM, tuple(min(w, tp) for w, _ in ATT_GROUPS), dils)
    pa_p, tails_p = rest[:3], rest[3:]
    pm_s, *rest = _inproj(xs, ada_s, n1g, w_all, SAMPLE_TM, (ns, ns, ns), (1, 1, 1))
    pa_s, new_t = rest[:3], rest[3:]

    sample_att = []
    for g, cache in enumerate((cache_win1, cache_win2, cache_win3)):
        cache_t = jnp.transpose(cache[0], (0, 2, 3, 4, 1))
        q = pa_s[g][0, 0, :, :ATT_GW].astype(F32).reshape(ns, ATT_HPG, ATT_E)
        sample_att.append((q, new_t[g][0], cache_t) + _sample_bias(rel_g[g], cache.shape[2], dils[g]))
    hosts = {"gla": bp * (tp // GLA_TC)}
    hosts.update({g: int(np.prod(_att_steps(pa_p[g])[0])) for g in groups})
    rider = {}
    for sg in reversed(groups):
        host = next((h for h, steps in hosts.items() if steps == ns and h not in rider), None)
        if host is not None:
            rider[host] = sg
    res_s = [None if g in rider.values() else _att_sample(*sample_att[g], g) for g in groups]
    side = lambda host: sample_att[rider[host]] if host in rider else None

    oa_p, gla_p, *rode = _gla_prompt(pm_p, wg2p, bg, gn, GLA_TC, side("gla"))
    if rode:
        res_s[rider["gla"]] = rode
    os_p, ls_p = [], []
    for g in groups:
        o, l, *rode = _att_prompt(pa_p[g], _prompt_bias(rel_g[g], dils[g]), g, side(g))
        if rode:
            res_s[rider[g]] = rode
        os_p.append(o)
        ls_p.append(l)
    y_p, cv_p = _mix_ffn(x_prompt, oa_p, os_p, ls_p, pm_p, ada_p, n2g, nfg, emat_t,
                         wa, wb, wo, wup, cw, cb, wdn, PROMPT_TM)

    oa_s, gla_s = _gla_sample(pm_s.reshape(ns, 1, MAIN_W), state_gla[0], wg2p, bg, gn)
    os_s, ls_s, new_caches = [], [], []
    for o, l, shifted in res_s:
        os_s.append(o.reshape(1, 1, ns, ATT_GW).astype(BF16))
        ls_s.append(jnp.pad(l[:, :, 0], ((0, 0), (0, LANE - ATT_HPG)))[None, None])
        new_caches.append(jnp.transpose(shifted, (0, 4, 1, 2, 3))[None])
    y_s, cv_s = _mix_ffn(xs, oa_s.reshape(1, ns, GLA_V_W), os_s, ls_s, pm_s, ada_s, n2g, nfg, emat_t,
                         wa, wb, wo, wup, cw, cb, wdn, SAMPLE_TM, conv_state=state_conv[0])

    tails = [jnp.transpose(c, (0, 4, 1, 2, 3))[None] for c in tails_p]
    conv_s = jnp.stack([state_conv[0][:, 1], cv_s[0]], axis=1)[None]
    return (y_p, y_s.reshape(ns, 1, D_MODEL), gla_p[None], gla_s[None],
            tails[0], new_caches[0], tails[1], new_caches[1], tails[2], new_caches[2],
            cv_p[:, 6:8][None], conv_s)
```

```python
import functools

import numpy as np
import jax
import jax.numpy as jnp
from jax import lax
from jax.experimental import pallas as pl
from jax.experimental.pallas import tpu as pltpu

F32 = jnp.float32
BF16 = jnp.bfloat16

D_MODEL = 1024
GLA_HEADS = 4
GLA_DK = 128
GLA_DV = 256
GLA_QK_W = GLA_HEADS * GLA_DK
GLA_V_W = GLA_HEADS * GLA_DV
GLA_GATE_RANK = 16
GLA_GATE_NORM = 16.0
GLA_CHUNK = 64
ATT_GROUPS = ((128, 1), (512, 4), (2048, 16))
ATT_HPG = 8
ATT_E = 64
ATT_GW = ATT_HPG * ATT_E
ATT_BLOCK = 128
ATT_SPAN = 128
REL_BUCKETS = 32
REL_MAX_DIST = 2048
FFN_HIDDEN = 2816
N_ADA = 6
EPS = 1e-6
NEG_BIG = -1e30

LANE = 128
GLR_PAD = LANE
MAIN_W = 2 * GLA_QK_W + 2 * GLA_V_W + 2 * D_MODEL + GLR_PAD
ATT_W3 = 3 * 3 * ATT_GW
W_ATT0 = 2 * GLA_QK_W + 2 * GLA_V_W
COL_CHUNK = 512
VMEM_LIMIT = 56 * 1024 * 1024

_NT = (((1,), (1,)), ((), ()))
_TN = (((0,), (0,)), ((), ()))


def _params(n_axes):
    return pltpu.CompilerParams(dimension_semantics=("arbitrary",) * n_axes,
                                vmem_limit_bytes=VMEM_LIMIT)


def _resident(shape):
    zeros = (0,) * len(shape)
    return pl.BlockSpec(shape, lambda *_: zeros, pipeline_mode=pl.Buffered(1))


def _silu(x):
    return x * jax.nn.sigmoid(x)


def _rms(x):
    return x * lax.rsqrt(jnp.mean(x * x, axis=-1, keepdims=True) + EPS)


def _split3(x):
    a = x.astype(BF16)
    r = x - a.astype(F32)
    b = r.astype(BF16)
    c = (r - b.astype(F32)).astype(BF16)
    return a, b, c


def _dot(a, b):
    return jnp.dot(a, b, preferred_element_type=F32)


def _ada_kernel(c_ref, w_ref, b_ref, o_ref):
    s = _silu(c_ref[...]).astype(BF16)
    o_ref[...] = _dot(s, w_ref[...]) + b_ref[...]


def _ada(c_all, w_ada, b_ada):
    m = c_all.shape[0]
    n = w_ada.shape[1]
    tn = D_MODEL
    return pl.pallas_call(
        _ada_kernel,
        grid=(n // tn,),
        in_specs=[pl.BlockSpec((m, D_MODEL), lambda j: (0, 0)),
                  pl.BlockSpec((D_MODEL, tn), lambda j: (0, j)),
                  pl.BlockSpec((1, tn), lambda j: (0, j))],
        out_specs=pl.BlockSpec((m, tn), lambda j: (0, j)),
        out_shape=jax.ShapeDtypeStruct((m, n), F32),
        compiler_params=_params(1),
        name="ada",
    )(c_all, w_ada, b_ada)


def _ada_spec(ada, tm, k):
    if ada.shape[1] == 1:
        return pl.BlockSpec((1, 1, D_MODEL), lambda b, i: (b, 0, k))
    return pl.BlockSpec((1, tm, D_MODEL), lambda b, i: (b, i, k))


def _inproj_kernel(x_ref, sh_ref, sc_ref, g_ref, w_ref, pm_ref, pa1_ref, pa2_ref, pa3_ref,
                   c1_ref, c2_ref, c3_ref, stage_ref):
    tm = x_ref.shape[1]
    h = _rms(x_ref[0]) * g_ref[...]
    h = h * (1.0 + sc_ref[0]) + sh_ref[0]
    hb = h.astype(BF16)

    def main_chunk(c0):
        c1 = min(c0 + COL_CHUNK, MAIN_W)
        w0 = c0 if c0 < W_ATT0 else c0 + ATT_W3
        pm_ref[0, :, c0:c1] = _dot(hb, w_ref[:, w0:w0 + c1 - c0]).astype(BF16)

    for g, (pa_ref, cache_ref) in reversed(list(enumerate(zip((pa1_ref, pa2_ref, pa3_ref),
                                                              (c1_ref, c2_ref, c3_ref))))):
        dil = pa_ref.shape[0]
        rows = cache_ref.shape[-1]
        for part in range(3):
            cs = slice(part * ATT_GW, (part + 1) * ATT_GW)
            c0 = W_ATT0 + part * 3 * ATT_GW + g * ATT_GW
            acc = _dot(hb, w_ref[:, c0:c0 + ATT_GW])
            if dil == 1:
                pa_ref[0, :, cs] = acc.astype(BF16)
            else:
                stage = stage_ref.at[3 * g + part]
                for c in range(ATT_GW // LANE):
                    stage[c] = acc[:, c * LANE:(c + 1) * LANE]
                for r in range(dil):
                    for c in range(ATT_GW // LANE):
                        c0 = part * ATT_GW + c * LANE
                        pa_ref[r, :, c0:c0 + LANE] = stage[c, pl.ds(r, tm // dil, stride=dil), :].astype(BF16)
            if part:
                tail = acc[tm - rows:, :].T
                cache_ref[part - 1] = tail.reshape(ATT_HPG, ATT_E, rows)
    for c0 in range(0, MAIN_W, COL_CHUNK):
        main_chunk(c0)


def _inproj(x, ada, norm_g, w_all, tm, windows, dils):
    bsz, t, _ = x.shape
    nt = t // tm
    pa_specs, pa_shapes, cache_specs, cache_shapes = [], [], [], []
    for w, dil in zip(windows, dils):
        rows = min(tm, w)
        nblk = w // rows
        pa_specs.append(pl.BlockSpec((None, dil, tm // dil, 3 * ATT_GW), lambda b, i: (b, 0, i, 0)))
        pa_shapes.append(jax.ShapeDtypeStruct((bsz, dil, t // dil, 3 * ATT_GW), BF16))
        cache_specs.append(pl.BlockSpec(
            (None, 2, ATT_HPG, ATT_E, rows),
            functools.partial(lambda b, i, first: (b, 0, 0, 0, jnp.maximum(i - first, 0)), first=nt - nblk)))
        cache_shapes.append(jax.ShapeDtypeStruct((bsz, 2, ATT_HPG, ATT_E, w), F32))
    return pl.pallas_call(
        _inproj_kernel,
        grid=(bsz, nt),
        in_specs=[pl.BlockSpec((1, tm, D_MODEL), lambda b, i: (b, i, 0)),
                  _ada_spec(ada, tm, 0), _ada_spec(ada, tm, 1),
                  _resident((1, D_MODEL)), _resident(w_all.shape)],
        out_specs=[pl.BlockSpec((1, tm, MAIN_W), lambda b, i: (b, i, 0))] + pa_specs + cache_specs,
        out_shape=[jax.ShapeDtypeStruct((bsz, t, MAIN_W), BF16)] + pa_shapes + cache_shapes,
        scratch_shapes=[pltpu.VMEM((3 * len(windows), ATT_GW // LANE, tm, LANE), F32)],
        compiler_params=_params(2),
        name="inproj",
    )(x, ada, ada, norm_g, w_all)


def _log_sigmoid(x):
    return jnp.minimum(x, 0.0) - jnp.log1p(jnp.exp(-jnp.abs(x)))


def _gla_kernel(qk_ref, v_ref, gr_ref, glr_ref, wg2_ref, bg_ref, gn_ref, tri_ref, *rest):
    scratch = rest[-7:]
    rest = rest[:-7]
    if len(rest) == 2:
        side_in, (o_ref, s_ref), side_out = (), rest, ()
    else:
        side_in, (o_ref, s_ref), side_out = rest[:5], rest[5:7], rest[7:10]
    st_ref, qi_ref, ki_ref, qb_ref, kd_ref, oraw_ref, u_ref = scratch
    t = pl.program_id(1)
    c = GLA_CHUNK
    tc = qk_ref.shape[1]
    n_chunks = tc // c

    @pl.when(t == 0)
    def _():
        st_ref[...] = jnp.zeros_like(st_ref)

    if side_in:
        _att_step(pl.program_id(0) * pl.num_programs(1) + t, *side_in, *side_out)

    logits = _dot(glr_ref[0], wg2_ref[...]) + bg_ref[...]
    g = _log_sigmoid(logits) * (1.0 / GLA_GATE_NORM)
    b = _dot_exact_lhs_t(tri_ref[...], g).reshape(n_chunks, c, GLA_QK_W)
    bref = b[:, c // 2:c // 2 + 1, :]
    blast = b[:, c - 1:c, :]
    qk = qk_ref[0].astype(F32).reshape(n_chunks, c, 2 * GLA_QK_W)
    q = qk[:, :, :GLA_QK_W] * (GLA_DK ** -0.5)
    k = qk[:, :, GLA_QK_W:]
    qi_ref[...] = (q * jnp.exp(b - bref)).astype(BF16)
    ki_ref[...] = (k * jnp.exp(bref - b)).astype(BF16)
    qb_ref[...] = (q * jnp.exp(b)).astype(BF16)
    kd_ref[...] = (k * jnp.exp(blast - b)).astype(BF16)
    e_last = jnp.exp(blast)

    row = lax.broadcasted_iota(jnp.int32, (c, c), 0)
    col = lax.broadcasted_iota(jnp.int32, (c, c), 1)
    tril = row >= col
    for ci in range(n_chunks):
        rows = slice(ci * c, (ci + 1) * c)
        for h in range(GLA_HEADS):
            ks = slice(h * GLA_DK, (h + 1) * GLA_DK)
            vs = slice(h * GLA_DV, (h + 1) * GLA_DV)
            vh = v_ref[0, rows, vs]
            a = lax.dot_general(qi_ref[ci, :, ks], ki_ref[ci, :, ks], _NT, preferred_element_type=F32)
            a = jnp.where(tril, a, 0.0)
            oraw_ref[rows, vs] = _dot(a.astype(BF16), vh)
            u_ref[ci, h] = lax.dot_general(vh, kd_ref[ci, :, ks], _TN, preferred_element_type=F32)
    for ci in range(n_chunks):
        rows = slice(ci * c, (ci + 1) * c)
        for h in range(GLA_HEADS):
            ks = slice(h * GLA_DK, (h + 1) * GLA_DK)
            vs = slice(h * GLA_DV, (h + 1) * GLA_DV)
            st = st_ref[h]
            oraw_ref[rows, vs] += lax.dot_general(
                qb_ref[ci, :, ks], st.astype(BF16), _NT, preferred_element_type=F32)
            st_ref[h] = st * e_last[ci, :, ks] + u_ref[ci, h]

    for h in range(GLA_HEADS):
        vs = slice(h * GLA_DV, (h + 1) * GLA_DV)
        o_ref[0, :, vs] = (_rms(oraw_ref[:, vs]) * gn_ref[...]
                           * _silu(gr_ref[0, :, vs].astype(F32))).astype(BF16)

    @pl.when(t == pl.num_programs(1) - 1)
    def _():
        for h in range(GLA_HEADS):
            s_ref[0, h] = st_ref[h].T


def _dot_exact_lhs_t(m, x):
    a, b, c = _split3(x)
    return _dot(m, a) + _dot(m, b) + _dot(m, c)


def _gla_prompt(pm, wg2p, bg, gn, tc, side=None):
    bsz, t, _ = pm.shape
    nt = t // tc
    blk = lambda k: pl.BlockSpec((1, tc, D_MODEL), functools.partial(lambda b, i, k: (b, i, k), k=k))
    pos = np.arange(tc)
    same_chunk = pos[:, None] // GLA_CHUNK == pos[None, :] // GLA_CHUNK
    tri = jnp.asarray(same_chunk & (pos[:, None] >= pos[None, :]), BF16)
    in_specs = [blk(0), blk(1), blk(2),
                pl.BlockSpec((1, tc, GLR_PAD), lambda b, i: (b, i, (MAIN_W - GLR_PAD) // GLR_PAD)),
                _resident(wg2p.shape), _resident(bg.shape), _resident(gn.shape), _resident(tri.shape)]
    out_specs = [pl.BlockSpec((1, tc, GLA_V_W), lambda b, i: (b, i, 0)),
                 pl.BlockSpec((1, GLA_HEADS, GLA_DK, GLA_DV), lambda b, i: (b, 0, 0, 0))]
    out_shape = [jax.ShapeDtypeStruct((bsz, t, GLA_V_W), BF16),
                 jax.ShapeDtypeStruct((bsz, GLA_HEADS, GLA_DK, GLA_DV), F32)]
    args = [pm, pm, pm, pm, wg2p, bg, gn, tri]
    staged = pltpu.VMEM((tc // GLA_CHUNK, GLA_CHUNK, GLA_QK_W), BF16)
    if side is not None:
        q, new_t, cache_t, sb, sb0 = side
        assert cache_t.shape[0] == bsz * nt, "one sample sequence per scan step"
        s_in, s_out, s_shape = _att_sample_specs(new_t, cache_t, sb, sb0, 1, lambda b, i: b * nt + i)
        in_specs += s_in
        out_specs += s_out
        out_shape += s_shape
        args += list(side)
    return pl.pallas_call(
        _gla_kernel,
        grid=(bsz, nt),
        in_specs=in_specs,
        out_specs=out_specs,
        out_shape=out_shape,
        scratch_shapes=[pltpu.VMEM((GLA_HEADS, GLA_DV, GLA_DK), F32), staged, staged, staged, staged,
                        pltpu.VMEM((tc, GLA_V_W), F32),
                        pltpu.VMEM((tc // GLA_CHUNK, GLA_HEADS, GLA_DV, GLA_DK), F32)],
        compiler_params=_params(2),
        name="gla_prompt",
    )(*args)


def _to_col(row):
    n = row.shape[1]
    eye = lax.broadcasted_iota(jnp.int32, (n, n), 0) == lax.broadcasted_iota(jnp.int32, (n, n), 1)
    return jnp.sum(jnp.where(eye, jnp.broadcast_to(row, (n, n)), 0.0), axis=1, keepdims=True)


GLA_SAMPLE_SEQS = 8


def _gla_step_kernel(pm_ref, s0_ref, wg2_ref, bg_ref, gn_ref, o_ref, s_ref):
    for j in range(pm_ref.shape[0]):
        p = pm_ref[j]
        glr = jnp.broadcast_to(p[:, MAIN_W - GLR_PAD:], (8, GLR_PAD))
        logits = _dot(glr, wg2_ref[...])[0:1] + bg_ref[...]
        eg = jnp.exp(_log_sigmoid(logits) * (1.0 / GLA_GATE_NORM))
        pf = p.astype(F32)
        q = pf[:, :GLA_QK_W] * (GLA_DK ** -0.5)
        k = pf[:, GLA_QK_W:2 * GLA_QK_W]
        v = pf[:, 2 * GLA_QK_W:2 * GLA_QK_W + GLA_V_W]
        gr = pf[:, 2 * GLA_QK_W + GLA_V_W:2 * GLA_QK_W + 2 * GLA_V_W]
        for h in range(GLA_HEADS):
            ks = slice(h * GLA_DK, (h + 1) * GLA_DK)
            vs = slice(h * GLA_DV, (h + 1) * GLA_DV)
            s_new = s0_ref[j, h] * _to_col(eg[:, ks]) + _to_col(k[:, ks]) * v[:, vs]
            s_ref[j, h] = s_new
            o = jnp.sum(_to_col(q[:, ks]) * s_new, axis=0, keepdims=True)
            o_ref[j, :, vs] = (_rms(o) * gn_ref[...] * _silu(gr[:, vs])).astype(BF16)


def _gla_sample(pm, s0, wg2p, bg, gn):
    n = pm.shape[0]
    seqs = GLA_SAMPLE_SEQS if n % GLA_SAMPLE_SEQS == 0 else 1
    st_spec = pl.BlockSpec((seqs, GLA_HEADS, GLA_DK, GLA_DV), lambda b: (b, 0, 0, 0))
    return pl.pallas_call(
        _gla_step_kernel,
        grid=(n // seqs,),
        in_specs=[pl.BlockSpec((seqs, 1, MAIN_W), lambda b: (b, 0, 0)), st_spec,
                  _resident(wg2p.shape), _resident(bg.shape), _resident(gn.shape)],
        out_specs=[pl.BlockSpec((seqs, 1, GLA_V_W), lambda b: (b, 0, 0)), st_spec],
        out_shape=[jax.ShapeDtypeStruct((n, 1, GLA_V_W), BF16),
                   jax.ShapeDtypeStruct(s0.shape, F32)],
        compiler_params=_params(1),
        name="gla_sample",
    )(pm, s0, wg2p, bg, gn)


ATT_MAX_STEP_BLOCKS = 4


def _att_kernel(q_ref, kp_ref, kc_ref, vp_ref, vc_ref, bias_ref, *rest):
    if len(rest) == 2:
        o_ref, lse_ref = rest
    else:
        o_ref, lse_ref = rest[5:7]
        step = (pl.program_id(0) * pl.num_programs(1) + pl.program_id(1)) * pl.num_programs(2) + pl.program_id(2)
        _att_step(step, *rest[:5], *rest[7:])
    kk = jnp.concatenate([kp_ref[...], kc_ref[...]], axis=0)
    vv = jnp.concatenate([vp_ref[...], vc_ref[...]], axis=0)
    lane = lax.broadcasted_iota(jnp.int32, (ATT_BLOCK, LANE), 1)
    low = lane < ATT_E
    scale = ATT_E ** -0.5
    half = (jnp.where(low, scale, 0.0).astype(BF16), jnp.where(low, 0.0, scale).astype(BF16))
    for blk in range(q_ref.shape[0] // ATT_BLOCK):
        first = jnp.where(pl.program_id(2) == 0, 1, 0) if blk == 0 else 0
        rows = slice(blk * ATT_BLOCK, (blk + 1) * ATT_BLOCK)
        keys = slice(blk * ATT_BLOCK, (blk + 2) * ATT_BLOCK)
        m_tile = jnp.zeros((ATT_BLOCK, LANE), F32)
        den_tile = jnp.ones((ATT_BLOCK, LANE), F32)
        for p in range(ATT_HPG // 2):
            cs = slice(p * LANE, (p + 1) * LANE)
            qp, kp, vp = q_ref[rows, cs], kk[keys, cs], vv[keys, cs]
            outs = []
            for hh in range(2):
                h = 2 * p + hh
                s = lax.dot_general(qp * half[hh], kp, _NT, preferred_element_type=F32)
                s = s + bias_ref[first, h]
                m = jnp.max(s, axis=-1, keepdims=True)
                e = jnp.exp(s - m)
                den = jnp.sum(e, axis=-1, keepdims=True)
                outs.append(_dot(e.astype(BF16), vp) / den)
                m_tile = jnp.where(lane == h, m, m_tile)
                den_tile = jnp.where(lane == h, den, den_tile)
            o_ref[rows, cs] = jnp.where(low, outs[0], outs[1]).astype(BF16)
        lse_ref[rows, :] = m_tile + jnp.log(den_tile)


def _att_steps(pa):
    bsz, dil, l, _ = pa.shape
    step_blocks = min(ATT_MAX_STEP_BLOCKS, l // ATT_BLOCK)
    return (bsz, dil, l // (step_blocks * ATT_BLOCK)), step_blocks


def _att_prompt(pa, bias2, g, side=None):
    bsz, dil, l, _ = pa.shape
    grid, step_blocks = _att_steps(pa)
    step_rows = step_blocks * ATT_BLOCK
    own = lambda part: pl.BlockSpec((None, None, step_rows, ATT_GW),
                                    functools.partial(lambda b, r, n, part: (b, r, n, part), part=part))
    prev = lambda part: pl.BlockSpec(
        (None, None, ATT_BLOCK, ATT_GW),
        functools.partial(lambda b, r, n, part: (b, r, jnp.maximum(step_blocks * n - 1, 0), part), part=part))
    out_spec = lambda width: pl.BlockSpec((None, None, step_rows, width), lambda b, r, n: (b, r, n, 0))
    in_specs = [own(0), prev(1), own(1), prev(2), own(2), _resident(bias2.shape)]
    out_specs = [out_spec(ATT_GW), out_spec(LANE)]
    out_shape = [jax.ShapeDtypeStruct((bsz, dil, l, ATT_GW), BF16),
                 jax.ShapeDtypeStruct((bsz, dil, l, LANE), F32)]
    args = [pa, pa, pa, pa, pa, bias2]
    if side is not None:
        new_t, cache_t, sb, sb0 = side[1:]
        assert cache_t.shape[0] == np.prod(grid), "one sample sequence per attention step"
        s_in, s_out, s_shape = _att_sample_specs(new_t, cache_t, sb, sb0, 1,
                                                 lambda b, r, n: (b * grid[1] + r) * grid[2] + n)
        in_specs += s_in
        out_specs += s_out
        out_shape += s_shape
        args += list(side)
    return pl.pallas_call(
        _att_kernel,
        grid=grid,
        in_specs=in_specs,
        out_specs=out_specs,
        out_shape=out_shape,
        compiler_params=_params(3),
        name=f"att_prompt_g{g}",
    )(*args)


def _att_step_kernel(q_ref, new_ref, buf_ref, sb_ref, sb0_ref, o_ref, l_ref, out_ref):
    _att_step(pl.program_id(0) * buf_ref.shape[0], q_ref, new_ref, buf_ref, sb_ref, sb0_ref, o_ref, l_ref, out_ref)


def _att_step(first_seq, q_ref, new_ref, buf_ref, sb_ref, sb0_ref, o_ref, l_ref, out_ref):
    seqs, _, _, _, lb = buf_ref.shape
    n = new_ref.shape[-1]
    scale = ATT_E ** -0.5
    head = lax.broadcasted_iota(jnp.int32, (ATT_HPG, 1), 0)
    last = lax.broadcasted_iota(jnp.int32, (1, lb), 1) == lb - 1
    for j in range(seqs):
        b = first_seq + j
        mine = lax.broadcasted_iota(jnp.int32, (1, n), 1) == b
        qb = q_ref[j].astype(BF16)
        s = jnp.zeros((ATT_HPG, lb), F32)
        sn = jnp.zeros((ATT_HPG, n), F32)
        for h in range(ATT_HPG):
            s = jnp.where(head == h, _dot(qb, buf_ref[j, 0, h].astype(BF16)), s)
            sn = jnp.where(head == h, _dot(qb, new_ref[0, h].astype(BF16)), sn)
        s = s * scale + sb_ref[...]
        sn = jnp.where(mine, sn * scale + sb0_ref[...], NEG_BIG)
        m = jnp.maximum(jnp.max(s, axis=-1, keepdims=True), jnp.max(sn, axis=-1, keepdims=True))
        e = jnp.exp(s - m)
        en = jnp.exp(sn - m)
        den = jnp.sum(e, axis=-1, keepdims=True) + jnp.sum(en, axis=-1, keepdims=True)
        pb = (e / den).astype(BF16)
        pnb = (en / den).astype(BF16)
        o = jnp.zeros((ATT_HPG, ATT_E), F32)
        for h in range(ATT_HPG):
            oh = (lax.dot_general(pb, buf_ref[j, 1, h].astype(BF16), _NT, preferred_element_type=F32)
                  + lax.dot_general(pnb, new_ref[1, h].astype(BF16), _NT, preferred_element_type=F32))
            o = jnp.where(head == h, oh, o)
        o_ref[j] = o
        l_ref[j] = jnp.broadcast_to(m + jnp.log(den), (ATT_HPG, LANE))
        for kv in range(2):
            for h in range(ATT_HPG):
                col = jnp.sum(jnp.where(mine, new_ref[kv, h], 0.0), axis=-1, keepdims=True)
                out_ref[j, kv, h] = jnp.where(last, col, pltpu.roll(buf_ref[j, kv, h], lb - 1, 1))


ATT_SAMPLE_STEP_BYTES = 8 * 1024 * 1024


def _att_sample(q, new_t, cache_t, sb, sb0, g):
    n, _, _, _, lb = cache_t.shape
    seqs = max(1, min(8, ATT_SAMPLE_STEP_BYTES // (2 * ATT_GW * lb * 4)))
    in_specs, out_specs, out_shape = _att_sample_specs(new_t, cache_t, sb, sb0, seqs, lambda i: i)
    return pl.pallas_call(
        _att_step_kernel,
        grid=(n // seqs,),
        in_specs=in_specs,
        out_specs=out_specs,
        out_shape=out_shape,
        compiler_params=_params(1),
        name=f"att_sample_g{g}",
    )(q, new_t, cache_t, sb, sb0)


def _att_sample_specs(new_t, cache_t, sb, sb0, seqs, block_of):
    n, _, _, _, lb = cache_t.shape
    planes = pl.BlockSpec((seqs, 2, ATT_HPG, ATT_E, lb), lambda *ids: (block_of(*ids), 0, 0, 0, 0))
    per_seq = lambda width: pl.BlockSpec((seqs, ATT_HPG, width), lambda *ids: (block_of(*ids), 0, 0))
    in_specs = [per_seq(ATT_E), _resident(new_t.shape), planes, _resident(sb.shape), _resident(sb0.shape)]
    out_specs = [per_seq(ATT_E), per_seq(LANE), planes]
    out_shape = [jax.ShapeDtypeStruct((n, ATT_HPG, ATT_E), F32),
                 jax.ShapeDtypeStruct((n, ATT_HPG, LANE), F32),
                 jax.ShapeDtypeStruct(cache_t.shape, F32)]
    return in_specs, out_specs, out_shape


def _mix_ffn_kernel(*refs, per_row_conv):
    (x_ref, oa_ref, o1_ref, o2_ref, o3_ref, l1_ref, l2_ref, l3_ref, ga_ref, gb_ref,
     g1_ref, sh2_ref, sc2_ref, g2_ref, n2_ref, nf_ref, et_ref,
     wa_ref, wb_ref, wo_ref, wup_ref, cw_ref, cb_ref, wdn_ref) = refs[:24]
    if per_row_conv:
        cs0_ref, cs1_ref, y_ref, cv_ref, so_ref, sl_ref = refs[24:]
    else:
        y_ref, cv_ref, so_ref, sl_ref, carry_ref = refs[24:]

        @pl.when(pl.program_id(1) == 0)
        def _():
            carry_ref[...] = jnp.zeros_like(carry_ref)
    tm = x_ref.shape[1]

    def token_order(ref, stage_ref):
        dil, _, width = ref.shape
        if dil == 1:
            return ref[0].astype(F32)
        for r in range(dil):
            plane = ref[r].astype(F32)
            for c in range(width // LANE):
                stage_ref[c, pl.ds(r, tm // dil, stride=dil), :] = plane[:, c * LANE:(c + 1) * LANE]
        return jnp.concatenate([stage_ref[c] for c in range(width // LANE)], axis=1)

    ls = [token_order(l_ref, sl_ref) for l_ref in (l1_ref, l2_ref, l3_ref)]
    m = jnp.maximum(jnp.maximum(ls[0], ls[1]), ls[2])
    ws = [jnp.exp(l - m) for l in ls]
    den = ws[0] + ws[1] + ws[2]
    o_att = jnp.zeros((tm, ATT_GW), F32)
    lane = lax.broadcasted_iota(jnp.int32, (tm, LANE), 1)
    for w, o_ref in zip(ws, (o1_ref, o2_ref, o3_ref)):
        wn = w / den
        hi = wn.astype(BF16).astype(F32)
        packed = jnp.where(lane < ATT_HPG, hi, pltpu.roll(wn - hi, ATT_HPG, 1)).astype(BF16)
        o_att = o_att + _dot(packed, et_ref[...]) * token_order(o_ref, so_ref)

    y_a = _dot(oa_ref[0], wa_ref[...])
    y_b = _dot(o_att.astype(BF16), wb_ref[...])
    mix_in = jax.nn.sigmoid(ga_ref[0].astype(F32)) * y_a + jax.nn.sigmoid(gb_ref[0].astype(F32)) * y_b
    x1 = x_ref[0] + g1_ref[0] * _dot(mix_in.astype(BF16), wo_ref[...])

    h2 = (_rms(x1) * n2_ref[...] * (1.0 + sc2_ref[0]) + sh2_ref[0]).astype(BF16)
    a = _dot(h2, wup_ref[:, :FFN_HIDDEN])
    bgate = _dot(h2, wup_ref[:, FFN_HIDDEN:])
    if per_row_conv:
        a2, a1 = cs0_ref[0], cs1_ref[0]
        cv_ref[0] = a
    else:
        row = lax.broadcasted_iota(jnp.int32, (tm, FFN_HIDDEN), 0)
        prev = carry_ref[...]
        a1 = jnp.where(row == 0, prev[7:8], pltpu.roll(a, 1, 0))
        a2 = jnp.where(row == 0, prev[6:7], jnp.where(row == 1, prev[7:8], pltpu.roll(a, 2, 0)))
        carry_ref[...] = a[tm - 8:, :]
        cv_ref[0] = a[tm - 8:, :]
    a_conv = cb_ref[...] + (cw_ref[0:1] * a2 + cw_ref[1:2] * a1 + cw_ref[2:3] * a)
    y_f = _dot((_silu(a_conv) * bgate).astype(BF16), wdn_ref[...])
    x2 = x1 + g2_ref[0] * y_f
    y_ref[0] = _rms(x2) * nf_ref[...]


def _mix_ffn(x, oa, os_, ls_, pm, ada, n2g, nfg, emat_t, wa, wb, wo, wup, cw, cb, wdn, tm, conv_state=None):
    bsz, t, _ = x.shape
    per_row = conv_state is not None
    rows = lambda width, k=0: pl.BlockSpec(
        (1, tm, width), functools.partial(lambda b, i, k: (b, i, k), k=k))
    planes = lambda a: pl.BlockSpec((None, a.shape[1], tm // a.shape[1], a.shape[3]),
                                    lambda b, i: (b, 0, i, 0))
    in_specs = ([rows(D_MODEL), rows(GLA_V_W)] + [planes(a) for a in (*os_, *ls_)]
                + [rows(D_MODEL, 3), rows(D_MODEL, 4)]
                + [_ada_spec(ada, tm, k) for k in (2, 3, 4, 5)]
                + [_resident(a.shape) for a in (n2g, nfg, emat_t, wa, wb, wo, wup, cw, cb, wdn)])
    args = [x, oa, *os_, *ls_, pm, pm, ada, ada, ada, ada, n2g, nfg, emat_t, wa, wb, wo, wup, cw, cb, wdn]
    scratch = [pltpu.VMEM((ATT_GW // LANE, tm, LANE), F32), pltpu.VMEM((1, tm, LANE), F32)]
    if per_row:
        in_specs += [rows(FFN_HIDDEN)] * 2
        args += [conv_state[:, 0][None], conv_state[:, 1][None]]
        cv_spec = rows(FFN_HIDDEN)
        cv_shape = jax.ShapeDtypeStruct((bsz, t, FFN_HIDDEN), F32)
    else:
        cv_spec = pl.BlockSpec((1, 8, FFN_HIDDEN), lambda b, i: (b, 0, 0))
        cv_shape = jax.ShapeDtypeStruct((bsz, 8, FFN_HIDDEN), F32)
        scratch.append(pltpu.VMEM((8, FFN_HIDDEN), F32))
    return pl.pallas_call(
        functools.partial(_mix_ffn_kernel, per_row_conv=per_row),
        grid=(bsz, t // tm),
        in_specs=in_specs,
        out_specs=[rows(D_MODEL), cv_spec],
        out_shape=[jax.ShapeDtypeStruct((bsz, t, D_MODEL), F32), cv_shape],
        scratch_shapes=scratch,
        compiler_params=_params(2),
        name="mix_ffn_sample" if per_row else "mix_ffn_prompt",
    )(*args)


def _t5_bucket(dist):
    max_exact = REL_BUCKETS // 2
    d = np.maximum(dist, 1).astype(np.float32)
    large = max_exact + (np.log(d / max_exact) / np.log(REL_MAX_DIST / max_exact)
                         * (REL_BUCKETS - max_exact)).astype(np.int32)
    large = np.minimum(large, REL_BUCKETS - 1)
    return np.where(dist < max_exact, dist, large).astype(np.int32)


def _prompt_bias(rel_bias_g, dil):
    i = np.arange(ATT_BLOCK)[:, None]
    j = np.arange(2 * ATT_BLOCK)[None, :]
    off = ATT_BLOCK + i - j
    valid = (off >= 0) & (off <= ATT_SPAN)
    by_off = rel_bias_g[_t5_bucket(np.arange(ATT_SPAN + 1) * dil)].T
    period = 4 * ATT_BLOCK
    seq = jnp.pad(by_off[:, ::-1], ((0, 0), (0, period - ATT_SPAN - 1)))
    rows = jnp.tile(seq, (1, ATT_BLOCK))[:, :ATT_BLOCK * (period - 1)]
    bias = rows.reshape(ATT_HPG, ATT_BLOCK, period - 1)[:, :, :2 * ATT_BLOCK]
    normal = jnp.where(valid[None], bias, NEG_BIG)
    first = jnp.where((valid & (j >= ATT_BLOCK))[None], bias, NEG_BIG)
    return jnp.stack([normal, first])


def _sample_bias(rel_bias_g, lb, dil):
    assert lb == ATT_SPAN * dil
    by_off = rel_bias_g[_t5_bucket(np.arange(ATT_SPAN + 1) * dil)].T
    strided = by_off[:, :0:-1, None]
    sb = jnp.pad(strided, ((0, 0), (0, 0), (0, dil - 1)), constant_values=NEG_BIG).reshape(ATT_HPG, lb)
    return sb, by_off[:, :1]


def _head_indicator_t():
    r = np.arange(LANE)[:, None]
    e = ((r < 2 * ATT_HPG) & (r % ATT_HPG == np.arange(ATT_GW)[None, :] // ATT_E)).astype(np.float32)
    return jnp.asarray(e, BF16)


def _permute_w_in(w_in):
    glr = jnp.pad(w_in[:, W_ATT0:W_ATT0 + GLA_GATE_RANK], ((0, 0), (0, GLR_PAD - GLA_GATE_RANK)))
    return jnp.concatenate([w_in[:, :W_ATT0], w_in[:, W_ATT0 + GLA_GATE_RANK:], glr], axis=1).astype(BF16)


PROMPT_TM = 256
SAMPLE_TM = 128
GLA_TC = 512


def kernel(x_prompt, x_sample, state_gla, cache_win1, cache_win2, cache_win3, state_conv, c_prompt, c_sample,
           w_ada, b_ada, norm1_g, w_in, gla_wg2, gla_bg, gla_norm_g, w_branch_a, w_branch_b, w_out, rel_bias,
           norm2_g, w_up, conv_w, conv_b, w_down, normf_g):
    bp, tp, _ = x_prompt.shape
    ns = x_sample.shape[0]
    assert w_ada.shape[0] == 1, "single-layer trunk"

    w_all = _permute_w_in(w_in[0])
    wg2p = jnp.pad(gla_wg2[0], ((0, GLR_PAD - GLA_GATE_RANK), (0, 0))).astype(BF16)
    bg = gla_bg[0][None]
    gn = gla_norm_g[0][None]
    n1g, n2g, nfg = norm1_g[0][None], norm2_g[0][None], normf_g[None]
    wa, wb, wo = w_branch_a[0].astype(BF16), w_branch_b[0].astype(BF16), w_out[0].astype(BF16)
    wup, wdn = w_up[0].astype(BF16), w_down[0].astype(BF16)
    cw, cb = conv_w[0], conv_b[0][None]
    emat_t = _head_indicator_t()
    groups = range(len(ATT_GROUPS))
    rel_g = [rel_bias[:, g * ATT_HPG:(g + 1) * ATT_HPG] for g in groups]

    ada = _ada(jnp.concatenate([c_prompt, c_sample], axis=0), w_ada[0].astype(BF16), b_ada[0][None])
    ada_p = ada[:bp][:, None, :]
    ada_s = ada[bp:][None]
    xs = x_sample.reshape(1, ns, D_MODEL)

    dils = tuple(d for _, d in ATT_GROUPS)
    pm_p, *rest = _inproj(x_prompt, ada_p, n1g, w_all, PROMPT_TM, tuple(min(w, tp) for w, _ in ATT_GROUPS), dils)
    pa_p, tails_p = rest[:3], rest[3:]
    pm_s, *rest = _inproj(xs, ada_s, n1g, w_all, SAMPLE_TM, (ns, ns, ns), (1, 1, 1))
    pa_s, new_t = rest[:3], rest[3:]

    sample_att = []
    for g, cache in enumerate((cache_win1, cache_win2, cache_win3)):
        cache_t = jnp.transpose(cache[0], (0, 2, 3, 4, 1))
        q = pa_s[g][0, 0, :, :ATT_GW].astype(F32).reshape(ns, ATT_HPG, ATT_E)
        sample_att.append((q, new_t[g][0], cache_t) + _sample_bias(rel_g[g], cache.shape[2], dils[g]))
    hosts = {"gla": bp * (tp // GLA_TC)}
    hosts.update({g: int(np.prod(_att_steps(pa_p[g])[0])) for g in groups})
    rider = {}
    for sg in reversed(groups):
        host = next((h for h, steps in hosts.items() if steps == ns and h not in rider), None)
        if host is not None:
            rider[host] = sg
    res_s = [None if g in rider.values() else _att_sample(*sample_att[g], g) for g in groups]
    side = lambda host: sample_att[rider[host]] if host in rider else None

    oa_p, gla_p, *rode = _gla_prompt(pm_p, wg2p, bg, gn, GLA_TC, side("gla"))
    if rode:
        res_s[rider["gla"]] = rode
    os_p, ls_p = [], []
    for g in groups:
        o, l, *rode = _att_prompt(pa_p[g], _prompt_bias(rel_g[g], dils[g]), g, side(g))
        if rode:
            res_s[rider[g]] = rode
        os_p.append(o)
        ls_p.append(l)
    y_p, cv_p = _mix_ffn(x_prompt, oa_p, os_p, ls_p, pm_p, ada_p, n2g, nfg, emat_t,
                         wa, wb, wo, wup, cw, cb, wdn, PROMPT_TM)

    oa_s, gla_s = _gla_sample(pm_s.reshape(ns, 1, MAIN_W), state_gla[0], wg2p, bg, gn)
    os_s, ls_s, new_caches = [], [], []
    for o, l, shifted in res_s:
        os_s.append(o.reshape(1, 1, ns, ATT_GW).astype(BF16))
        ls_s.append(jnp.pad(l[:, :, 0], ((0, 0), (0, LANE - ATT_HPG)))[None, None])
        new_caches.append(jnp.transpose(shifted, (0, 4, 1, 2, 3))[None])
    y_s, cv_s = _mix_ffn(xs, oa_s.reshape(1, ns, GLA_V_W), os_s, ls_s, pm_s, ada_s, n2g, nfg, emat_t,
                         wa, wb, wo, wup, cw, cb, wdn, SAMPLE_TM, conv_state=state_conv[0])

    tails = [jnp.transpose(c, (0, 4, 1, 2, 3))[None] for c in tails_p]
    conv_s = jnp.stack([state_conv[0][:, 1], cv_s[0]], axis=1)[None]
    return (y_p, y_s.reshape(ns, 1, D_MODEL), gla_p[None], gla_s[None],
            tails[0], new_caches[0], tails[1], new_caches[1], tails[2], new_caches[2],
            cv_p[:, 6:8][None], conv_s)
```
